```python
import math
import jax, jax.numpy as jnp
from jax import lax
import numpy as np

D_MODEL = 2048
BATCH = 1
SEQ = 8192
DEPTH = 4

N_MIXERS = 3
EPS = 1e-6
DA_HEADS = 8
DA_QK_DIM = 128
DA_V_DIM = 2 * DA_QK_DIM
DA_WIDTH = DA_HEADS * DA_V_DIM
DA_IN = 4 * DA_HEADS * DA_QK_DIM + 2 * DA_WIDTH
Q_BLOCK = 128
HG_HEADS = 16
HG_KEY_DIM = 128
HG_VAL_DIM = D_MODEL // HG_HEADS
HG_FDIM = HG_HEADS * HG_KEY_DIM
HG_WIDTH = HG_HEADS * HG_VAL_DIM
HG_IN = 2 * HG_FDIM + 2 * HG_WIDTH
HG_CHUNK = 64
CONV_WIDTH = 31
CV_WIDTH = D_MODEL
CV_IN = 3 * CV_WIDTH

kernel_name = "hybrid_diffattn_hgrn2_conformer_adaln"


def _count(kind):
    return len(range(kind, DEPTH, N_MIXERS))


def rmsnorm(x, gain):
    xf = x.astype(jnp.float32)
    y = xf * lax.rsqrt(jnp.mean(xf * xf, axis=-1, keepdims=True) + EPS)
    return (y * gain.astype(jnp.float32)).astype(x.dtype)


def layernorm(x, gain, bias):
    xf = x.astype(jnp.float32)
    mu = jnp.mean(xf, axis=-1, keepdims=True)
    var = jnp.mean(jnp.square(xf - mu), axis=-1, keepdims=True)
    y = (xf - mu) * lax.rsqrt(var + EPS)
    return (y * gain.astype(jnp.float32) + bias.astype(jnp.float32)).astype(x.dtype)


def diff_attention(h, w_in, w_out, lam_q1, lam_k1, lam_q2, lam_k2, subln_g, layer_idx):
    B, S, _ = h.shape
    H, dk, dv = DA_HEADS, DA_QK_DIM, DA_V_DIM
    u = h @ w_in
    q, k, v, gate = jnp.split(u, [2 * H * dk, 4 * H * dk, 4 * H * dk + DA_WIDTH], axis=-1)
    q = q.reshape(B, S, H, 2, dk).transpose(0, 2, 3, 1, 4) * (dk ** -0.5)
    k = k.reshape(B, S, H, 2, dk).transpose(0, 2, 3, 1, 4)
    v = v.reshape(B, S, H, dv).transpose(0, 2, 1, 3)
    lam_init = 0.8 - 0.6 * math.exp(-0.3 * layer_idx)
    lam = (jnp.exp(jnp.sum(lam_q1.astype(jnp.float32) * lam_k1.astype(jnp.float32)))
           - jnp.exp(jnp.sum(lam_q2.astype(jnp.float32) * lam_k2.astype(jnp.float32)))
           + lam_init)
    slopes = 2.0 ** (-8.0 * jnp.arange(1, H + 1, dtype=jnp.float32) / H)
    n_blk = S // Q_BLOCK
    q_blocks = q.reshape(B, H, 2, n_blk, Q_BLOCK, dk).transpose(3, 0, 1, 2, 4, 5)
    key_pos = jnp.arange(S)

    def block(args):
        qb, blk = args
        q_pos = blk * Q_BLOCK + jnp.arange(Q_BLOCK)
        s = jnp.einsum('bhmqd,bhmkd->bhmqk', qb, k).astype(jnp.float32)
        dist = (q_pos[:, None] - key_pos[None, :]).astype(jnp.float32)
        s = s - slopes[None, :, None, None, None] * dist
        s = jnp.where(dist >= 0, s, -jnp.inf)
        p = jax.nn.softmax(s, axis=-1)
        attn = p[:, :, 0] - lam * p[:, :, 1]
        return jnp.einsum('bhqk,bhkd->bhqd', attn.astype(v.dtype), v)

    o = lax.map(block, (q_blocks, jnp.arange(n_blk)))
    o = o.transpose(1, 0, 3, 2, 4).reshape(B, S, H, dv)
    o = rmsnorm(o, subln_g) * (1.0 - lam_init)
    o = o.reshape(B, S, DA_WIDTH) * jax.nn.silu(gate)
    return o @ w_out


def hgrn2(h, w_in, w_out, lb_logits, gnorm_g, layer_idx):
    B, S, _ = h.shape
    H, dk, dv, C = HG_HEADS, HG_KEY_DIM, HG_VAL_DIM, HG_CHUNK
    u = h @ w_in
    q, f, i_in, gate = jnp.split(u, [HG_FDIM, 2 * HG_FDIM, 2 * HG_FDIM + HG_WIDTH], axis=-1)
    lb = jax.nn.softmax(lb_logits.astype(jnp.float32), axis=0)
    lb = (jnp.cumsum(lb, axis=0) - lb[0])[layer_idx]
    f_gate = lb + (1.0 - lb) * jax.nn.sigmoid(f.astype(jnp.float32))
    kk = 1.0 - f_gate
    g = jnp.log(f_gate)
    qf = jax.nn.silu(q.astype(jnp.float32))
    vf = i_in.astype(jnp.float32)
    nC = S // C

    def to_chunks(t, d):
        return t.reshape(B, nC, C, H, d).transpose(1, 0, 3, 2, 4)

    qc, kc, gc, vc = to_chunks(qf, dk), to_chunks(kk, dk), to_chunks(g, dk), to_chunks(vf, dv)
    causal = jnp.tril(jnp.ones((C, C), dtype=bool))

    def step(state, inp):
        q_, k_, g_, v_ = inp
        b = jnp.cumsum(g_, axis=-2)
        diff = b[:, :, :, None, :] - b[:, :, None, :, :]
        decay = jnp.exp(jnp.where(causal[:, :, None], diff, -jnp.inf))
        a = jnp.einsum('bhtd,bhsd,bhtsd->bhts', q_, k_, decay)
        o = jnp.einsum('bhts,bhse->bhte', a, v_) + jnp.einsum('bhtd,bhde->bhte', q_ * jnp.exp(b), state)
        b_last = b[:, :, -1:, :]
        state = (jnp.exp(b_last[:, :, 0, :])[..., None] * state
                 + jnp.einsum('bhsd,bhse->bhde', k_ * jnp.exp(b_last - b), v_))
        return state, o

    s0 = jnp.zeros((B, H, dk, dv), jnp.float32)
    _, o = lax.scan(step, s0, (qc, kc, gc, vc))
    o = o.transpose(1, 0, 3, 2, 4).reshape(B, S, HG_WIDTH).astype(h.dtype)
    o = rmsnorm(o, gnorm_g) * jax.nn.silu(gate)
    return o @ w_out


def conv_module(h, w_in, b_in, dw_kernel, dw_bias, ln_g, ln_b, w_out, b_out):
    u = h @ w_in + b_in
    a, a_gate, gate = jnp.split(u, 3, axis=-1)
    y = a * jax.nn.sigmoid(a_gate)
    y = lax.conv_general_dilated(
        y, dw_kernel[:, None, :].astype(y.dtype), window_strides=(1,),
        padding=[(CONV_WIDTH - 1, 0)], dimension_numbers=('NWC', 'WIO', 'NWC'),
        feature_group_count=CV_WIDTH) + dw_bias
    y = layernorm(y, ln_g, ln_b)
    y = jax.nn.silu(y) * jax.nn.silu(gate)
    return y @ w_out + b_out


def setup_inputs(seed: int = 0) -> dict:
    key = jax.random.key(seed)
    ks = jax.random.split(key, 32)
    D = D_MODEL
    nA, nB, nC = _count(0), _count(1), _count(2)
    nrm = jax.random.normal
    f32 = jnp.float32
    return {
        "x": nrm(ks[0], (BATCH, SEQ, D), f32),
        "c": nrm(ks[1], (BATCH, D), f32),
        "norm_g": 1.0 + 0.02 * nrm(ks[2], (DEPTH, D), f32),
        "ada_w": nrm(ks[3], (DEPTH, D, 3 * D), f32) * D ** -0.5,
        "ada_b": 0.01 * nrm(ks[4], (DEPTH, 3 * D), f32),
        "attn_w_in": nrm(ks[5], (nA, D, DA_IN), f32) * D ** -0.5,
        "attn_w_out": nrm(ks[6], (nA, DA_WIDTH, D), f32) * DA_WIDTH ** -0.5,
        "attn_lam_q1": 0.1 * nrm(ks[7], (nA, DA_QK_DIM), f32),
        "attn_lam_k1": 0.1 * nrm(ks[8], (nA, DA_QK_DIM), f32),
        "attn_lam_q2": 0.1 * nrm(ks[9], (nA, DA_QK_DIM), f32),
        "attn_lam_k2": 0.1 * nrm(ks[10], (nA, DA_QK_DIM), f32),
        "attn_subln_g": 1.0 + 0.02 * nrm(ks[11], (nA, DA_V_DIM), f32),
        "hgrn_w_in": nrm(ks[12], (nB, D, HG_IN), f32) * D ** -0.5,
        "hgrn_w_out": nrm(ks[13], (nB, HG_WIDTH, D), f32) * HG_WIDTH ** -0.5,
        "hgrn_lb_logits": 0.1 * nrm(ks[14], (DEPTH, HG_FDIM), f32),
        "hgrn_gnorm_g": 1.0 + 0.02 * nrm(ks[15], (nB, HG_WIDTH), f32),
        "conv_w_in": nrm(ks[16], (nC, D, CV_IN), f32) * D ** -0.5,
        "conv_b_in": 0.01 * nrm(ks[17], (nC, CV_IN), f32),
        "conv_dw": nrm(ks[18], (nC, CONV_WIDTH, CV_WIDTH), f32) * CONV_WIDTH ** -0.5,
        "conv_dw_b": 0.01 * nrm(ks[19], (nC, CV_WIDTH), f32),
        "conv_ln_g": 1.0 + 0.02 * nrm(ks[20], (nC, CV_WIDTH), f32),
        "conv_ln_b": 0.01 * nrm(ks[21], (nC, CV_WIDTH), f32),
        "conv_w_out": nrm(ks[22], (nC, CV_WIDTH, D), f32) * CV_WIDTH ** -0.5,
        "conv_b_out": 0.01 * nrm(ks[23], (nC, D), f32),
        "final_g": 1.0 + 0.02 * nrm(ks[24], (D,), f32),
    }


def reference(x, c, norm_g, ada_w, ada_b,
              attn_w_in, attn_w_out, attn_lam_q1, attn_lam_k1, attn_lam_q2, attn_lam_k2, attn_subln_g,
              hgrn_w_in, hgrn_w_out, hgrn_lb_logits, hgrn_gnorm_g,
              conv_w_in, conv_b_in, conv_dw, conv_dw_b, conv_ln_g, conv_ln_b, conv_w_out, conv_b_out,
              final_g):
    c_act = jax.nn.silu(c)
    for i in range(DEPTH):
        mod = c_act @ ada_w[i] + ada_b[i]
        shift, scale, gate = jnp.split(mod[:, None, :], 3, axis=-1)
        h = rmsnorm(x, norm_g[i]) * (1.0 + scale) + shift
        kind, j = i % N_MIXERS, i // N_MIXERS
        if kind == 0:
            y = diff_attention(h, attn_w_in[j], attn_w_out[j], attn_lam_q1[j], attn_lam_k1[j],
                               attn_lam_q2[j], attn_lam_k2[j], attn_subln_g[j], i)
        elif kind == 1:
            y = hgrn2(h, hgrn_w_in[j], hgrn_w_out[j], hgrn_lb_logits, hgrn_gnorm_g[j], i)
        else:
            y = conv_module(h, conv_w_in[j], conv_b_in[j], conv_dw[j], conv_dw_b[j],
                            conv_ln_g[j], conv_ln_b[j], conv_w_out[j], conv_b_out[j])
        x = x + gate * y
    return rmsnorm(x, final_g)
```

```python
import functools
import math

import numpy as np
import jax
import jax.numpy as jnp
from jax import lax
from jax.experimental import pallas as pl
from jax.experimental.pallas import tpu as pltpu

F32 = jnp.float32
BF16 = jnp.bfloat16

EPS = 1e-6
LOG2E = 1.4426950408889634
N_MIXERS = 3

DA_HEADS = 8
DA_QK_DIM = 128
DA_V_DIM = 256
HG_HEADS = 16
HG_DIM = 128
HG_CHUNK = 64
HG_LEVELS = (32, 16, 8)
CONV_WIDTH = 31
CONV_HALO = 32

VMEM_LIMIT_BYTES = 56 * 1024 * 1024


def _cparams(*sem):
    return pltpu.CompilerParams(dimension_semantics=sem, vmem_limit_bytes=VMEM_LIMIT_BYTES)


def _silu(x):
    return x * jax.nn.sigmoid(x)


def _mods_kernel(c_ref, w_ref, b_ref, o_ref, ca_ref, *, kc):
    c = c_ref[...]
    ca_ref[...] = _silu(c)
    d = c.shape[0]
    tn = w_ref.shape[2]

    def body(i, acc):
        k0 = pl.multiple_of(i * kc, kc)
        prod = w_ref[0, pl.ds(k0, kc), :] * ca_ref[pl.ds(k0, kc), :]
        return acc + prod.reshape(kc // 8, 8, tn).sum(axis=0)

    acc = lax.fori_loop(0, d // kc, body, jnp.zeros((8, tn), F32))
    o_ref[0] = jnp.sum(acc, axis=0, keepdims=True) + b_ref[0]


def _mods(c, ada_w, ada_b, *, tn=1024, kc=64):
    depth, d, n = ada_w.shape
    return pl.pallas_call(
        functools.partial(_mods_kernel, kc=kc),
        grid=(depth, n // tn),
        in_specs=[
            pl.BlockSpec((d, 1), lambda i, j: (0, 0)),
            pl.BlockSpec((1, d, tn), lambda i, j: (i, 0, j)),
            pl.BlockSpec((1, 1, tn), lambda i, j: (i, 0, j)),
        ],
        out_specs=pl.BlockSpec((1, 1, tn), lambda i, j: (i, 0, j)),
        out_shape=jax.ShapeDtypeStruct((depth, 1, n), F32),
        scratch_shapes=[pltpu.VMEM((d, 1), F32)],
        compiler_params=_cparams("arbitrary", "arbitrary"),
        name="adaln_mods",
    )(c.reshape(d, 1), ada_w, ada_b.reshape(depth, 1, n))


def _norm_kernel(x_ref, g_ref, sc_ref, sh_ref, o_ref, *, modulate):
    x = x_ref[...]
    y = x * lax.rsqrt(jnp.mean(x * x, axis=-1, keepdims=True) + EPS) * g_ref[...]
    if modulate:
        y = y * (1.0 + sc_ref[...]) + sh_ref[...]
    o_ref[...] = y.astype(o_ref.dtype)


def _norm(x, gain, scale, shift, *, modulate, out_dtype, tm=512):
    s, d = x.shape
    row = pl.BlockSpec((1, d), lambda i: (0, 0))
    return pl.pallas_call(
        functools.partial(_norm_kernel, modulate=modulate),
        grid=(s // tm,),
        in_specs=[pl.BlockSpec((tm, d), lambda i: (i, 0)), row, row, row],
        out_specs=pl.BlockSpec((tm, d), lambda i: (i, 0)),
        out_shape=jax.ShapeDtypeStruct((s, d), out_dtype),
        compiler_params=_cparams("arbitrary"),
        name="rmsnorm_mod",
    )(x, gain, scale, shift)


def _mm_kernel(*refs, has_bias, has_colscale, has_res):
    a_ref, w_ref = refs[0], refs[1]
    o_ref = refs[-1]
    pos = 2
    acc = jnp.dot(a_ref[...], w_ref[...], preferred_element_type=F32)
    if has_bias:
        acc = acc + refs[pos][...]
        pos += 1
    if has_colscale:
        acc = acc * refs[pos][...]
        pos += 1
    if has_res:
        acc = refs[pos][...] + refs[pos + 1][...] * acc
    o_ref[...] = acc.astype(o_ref.dtype)


def _matmul(a, w, *, col0, n_out, out_dtype, bias=None, colscale=None, res=None, res_gate=None,
            tm=512, tn=1024):
    m, k = a.shape
    off = col0 // tn
    in_specs = [
        pl.BlockSpec((tm, k), lambda j, i: (i, 0)),
        pl.BlockSpec((k, tn), lambda j, i: (0, j + off)),
    ]
    args = [a, w]
    rowspec = pl.BlockSpec((1, tn), lambda j, i: (0, j))
    if bias is not None:
        in_specs.append(rowspec)
        args.append(bias)
    if colscale is not None:
        in_specs.append(rowspec)
        args.append(colscale)
    if res is not None:
        in_specs += [pl.BlockSpec((tm, tn), lambda j, i: (i, j)), rowspec]
        args += [res, res_gate]
    return pl.pallas_call(
        functools.partial(_mm_kernel, has_bias=bias is not None, has_colscale=colscale is not None,
                          has_res=res is not None),
        grid=(n_out // tn, m // tm),
        in_specs=in_specs,
        out_specs=pl.BlockSpec((tm, tn), lambda j, i: (i, j)),
        out_shape=jax.ShapeDtypeStruct((m, n_out), out_dtype),
        compiler_params=_cparams("arbitrary", "arbitrary"),
        name="matmul",
    )(*args)


def _attn_kernel(slope_ref, lam_ref, q_ref, k_ref, v_ref, g_ref, sg_ref, o_ref, acc_ref,
                 *, tq, out_scale):
    h = pl.program_id(0)
    i = pl.program_id(1)
    tk = tq
    dk = DA_QK_DIM
    slope = slope_ref[h]
    lam = lam_ref[0]
    acc_ref[...] = jnp.zeros_like(acc_ref)
    colf = lax.broadcasted_iota(jnp.int32, (1, tk), 1).astype(F32)

    def step(j, carry, masked):
        k0 = pl.multiple_of(j * tk, tk)
        kb = k_ref[pl.ds(k0, tk), :]
        vb = v_ref[pl.ds(k0, tk), :]
        bias = slope * (colf + ((j - i) * tk).astype(F32))
        if masked:
            r = lax.broadcasted_iota(jnp.int32, (tq, tk), 0)
            c = lax.broadcasted_iota(jnp.int32, (tq, tk), 1)
            keep = r >= c
        out = []
        for mi in range(2):
            m, l = carry[2 * mi], carry[2 * mi + 1]
            qm = q_ref[:, mi * dk:(mi + 1) * dk]
            km = kb[:, mi * dk:(mi + 1) * dk]
            s = lax.dot_general(qm, km, (((1,), (1,)), ((), ())), preferred_element_type=F32) + bias
            if masked:
                s = jnp.where(keep, s, -jnp.inf)
            m_new = jnp.maximum(m, jnp.max(s, axis=-1, keepdims=True))
            alpha = jnp.exp2(m - m_new)
            p = jnp.exp2(s - m_new)
            l_new = alpha * l + jnp.sum(p, axis=-1, keepdims=True)
            acc_ref[mi] = alpha * acc_ref[mi] + jnp.dot(p.astype(BF16), vb, preferred_element_type=F32)
            out += [m_new, l_new]
        return tuple(out)

    neg = jnp.full((tq, 1), -1e30, F32)
    zero = jnp.zeros((tq, 1), F32)
    carry = lax.fori_loop(0, i, functools.partial(step, masked=False), (neg, zero, neg, zero))
    _, l1, _, l2 = step(i, carry, True)

    o = acc_ref[0] / l1 - lam * (acc_ref[1] / l2)
    y = o * lax.rsqrt(jnp.mean(o * o, axis=-1, keepdims=True) + EPS) * sg_ref[...] * out_scale
    o_ref[...] = (y * _silu(g_ref[...])).astype(o_ref.dtype)


def _attention(qkv, gate, slopes, lam, subln_g, *, out_scale, tq=512):
    s = qkv.shape[0]
    dv = DA_V_DIM
    nh = DA_HEADS
    smem = pl.BlockSpec(memory_space=pltpu.SMEM)
    return pl.pallas_call(
        functools.partial(_attn_kernel, tq=tq, out_scale=out_scale),
        grid=(nh, s // tq),
        in_specs=[
            smem, smem,
            pl.BlockSpec((tq, dv), lambda h, i: (i, h)),
            pl.BlockSpec((s, dv), lambda h, i: (0, nh + h)),
            pl.BlockSpec((s, dv), lambda h, i: (0, 2 * nh + h)),
            pl.BlockSpec((tq, dv), lambda h, i: (i, h)),
            pl.BlockSpec((1, dv), lambda h, i: (0, 0)),
        ],
        out_specs=pl.BlockSpec((tq, dv), lambda h, i: (i, h)),
        out_shape=jax.ShapeDtypeStruct((s, nh * dv), BF16),
        scratch_shapes=[pltpu.VMEM((2, tq, dv), F32)],
        compiler_params=_cparams("arbitrary", "arbitrary"),
        name="diff_attention",
    )(slopes, lam, qkv, qkv, qkv, gate, subln_g)


def _hgrn_exponent_matrix():
    c = HG_CHUNK
    w = np.zeros((len(HG_LEVELS) + 1, c, c), np.float32)
    for r in range(c):
        w[0, r, :r + 1] = 1.0
        for li, m in enumerate(HG_LEVELS):
            mid = (r // (2 * m)) * 2 * m + m
            if r % (2 * m) >= m:
                w[li + 1, r, mid:r + 1] = 1.0
            else:
                w[li + 1, r, r + 1:mid] = 1.0
    return w.reshape(-1, c)


def _hgrn_kernel(q_ref, f_ref, i_ref, lb_ref, w_ref, o_ref, st_ref, *, tt):
    c = HG_CHUNK

    @pl.when(pl.program_id(1) == 0)
    def _():
        st_ref[...] = jnp.zeros_like(st_ref)

    lb = lb_ref[...]
    wall = w_ref[...]
    row = lax.broadcasted_iota(jnp.int32, (c, 1), 0)
    r2 = lax.broadcasted_iota(jnp.int32, (c, c), 0)
    c2 = lax.broadcasted_iota(jnp.int32, (c, c), 1)
    row8 = lax.broadcasted_iota(jnp.int32, (8, 1), 0)
    nt = (((1,), (1,)), ((), ()))
    tn = (((0,), (0,)), ((), ()))

    def chunk(ci, carry):
        r0 = pl.multiple_of(ci * c, c)
        q = _silu(q_ref[pl.ds(r0, c), :])
        v = i_ref[pl.ds(r0, c), :]
        fg = lb + (1.0 - lb) * jax.nn.sigmoid(f_ref[pl.ds(r0, c), :])
        kk = 1.0 - fg
        g = jnp.log(fg)
        g_hi = g.astype(BF16)
        g_lo = (g - g_hi.astype(F32)).astype(BF16)
        ex = (jnp.dot(wall, g_hi, preferred_element_type=F32)
              + jnp.dot(wall, g_lo, preferred_element_type=F32))
        b = ex[0:c]
        st = st_ref[...]
        vb = v.astype(BF16)

        o = lax.dot_general((q * jnp.exp(b)).astype(BF16), st.astype(BF16), nt,
                            preferred_element_type=F32)

        a = jnp.zeros((c, c), F32)
        for li, m in enumerate(HG_LEVELS):
            e = jnp.exp(ex[(li + 1) * c:(li + 2) * c])
            upper = (row % (2 * m)) >= m
            qt = jnp.where(upper, q * e, 0.0).astype(BF16)
            kt = jnp.where(upper, 0.0, kk * e).astype(BF16)
            al = lax.dot_general(qt, kt, nt, preferred_element_type=F32)
            if 2 * m != c:
                al = jnp.where((r2 // (2 * m)) == (c2 // (2 * m)), al, 0.0)
            a = a + al
        o = o + jnp.dot(a.astype(BF16), vb, preferred_element_type=F32)

        parts = []
        for blk in range(c // 8):
            sl = slice(8 * blk, 8 * blk + 8)
            bb, kb, v8, q8 = b[sl], kk[sl], v[sl], q[sl]
            od = jnp.zeros((8, HG_DIM), F32)
            for s in range(8):
                e = jnp.exp(jnp.where(row8 >= s, bb - bb[s:s + 1], -jnp.inf))
                w = jnp.sum(e * q8 * kb[s:s + 1], axis=-1, keepdims=True)
                od = od + w * v8[s:s + 1]
            parts.append(od)
        o_ref[pl.ds(r0, c), :] = o + jnp.concatenate(parts, axis=0)

        b_last = b[c - 1:c]
        kd = (kk * jnp.exp(b_last - b)).astype(BF16)
        st_ref[...] = jnp.exp(b_last) * st + lax.dot_general(vb, kd, tn, preferred_element_type=F32)
        return carry

    lax.fori_loop(0, tt // c, chunk, 0)


def _hgrn_core(u, lb, *, tt=512):
    s = u.shape[0]
    nh, dk = HG_HEADS, HG_DIM
    wall = jnp.asarray(_hgrn_exponent_matrix(), BF16)
    return pl.pallas_call(
        functools.partial(_hgrn_kernel, tt=tt),
        grid=(nh, s // tt),
        in_specs=[
            pl.BlockSpec((tt, dk), lambda h, t: (t, h)),
            pl.BlockSpec((tt, dk), lambda h, t: (t, nh + h)),
            pl.BlockSpec((tt, dk), lambda h, t: (t, 2 * nh + h)),
            pl.BlockSpec((1, dk), lambda h, t: (0, h)),
            pl.BlockSpec(wall.shape, lambda h, t: (0, 0)),
        ],
        out_specs=pl.BlockSpec((tt, dk), lambda h, t: (t, h)),
        out_shape=jax.ShapeDtypeStruct((s, nh * dk), F32),
        scratch_shapes=[pltpu.VMEM((dk, dk), F32)],
        compiler_params=_cparams("arbitrary", "arbitrary"),
        name="hgrn2_core",
    )(u, u, u, lb, wall)


def _gnorm_kernel(o_ref, gate_ref, g_ref, out_ref):
    o = o_ref[...]
    y = o * lax.rsqrt(jnp.mean(o * o, axis=-1, keepdims=True) + EPS) * g_ref[...]
    out_ref[...] = (y * _silu(gate_ref[...])).astype(out_ref.dtype)


def _gnorm_gate(o, u, gain, *, tm=512):
    s, d = o.shape
    gate_blk = (u.shape[1] - d) // d
    return pl.pallas_call(
        _gnorm_kernel,
        grid=(s // tm,),
        in_specs=[
            pl.BlockSpec((tm, d), lambda i: (i, 0)),
            pl.BlockSpec((tm, d), lambda i: (i, gate_blk)),
            pl.BlockSpec((1, d), lambda i: (0, 0)),
        ],
        out_specs=pl.BlockSpec((tm, d), lambda i: (i, 0)),
        out_shape=jax.ShapeDtypeStruct((s, d), BF16),
        compiler_params=_cparams("arbitrary"),
        name="hgrn2_gnorm_gate",
    )(o, u, gain)


def _conv_kernel(a_ref, ag_ref, gate_ref, dw_ref, dwb_ref, lng_ref, lnb_ref, o_ref, ybuf, cbuf,
                 *, tt, cw):
    halo = CONV_HALO
    d = a_ref.shape[1]

    @pl.when(pl.program_id(0) == 0)
    def _():
        ybuf[0:halo, :] = jnp.zeros((halo, d), F32)

    @pl.when(pl.program_id(0) > 0)
    def _():
        ybuf[0:halo, :] = ybuf[tt:tt + halo, :]

    ybuf[halo:halo + tt, :] = a_ref[...] * jax.nn.sigmoid(ag_ref[...])
    first = halo - (CONV_WIDTH - 1)

    def lanes(ci, carry):
        c0 = pl.multiple_of(ci * cw, cw)
        acc = jnp.broadcast_to(dwb_ref[:, pl.ds(c0, cw)], (tt, cw))
        for k in range(CONV_WIDTH):
            acc = acc + ybuf[first + k:first + k + tt, pl.ds(c0, cw)] * dw_ref[k:k + 1, pl.ds(c0, cw)]
        cbuf[:, pl.ds(c0, cw)] = acc
        return carry

    lax.fori_loop(0, d // cw, lanes, 0)

    y = cbuf[...]
    mu = jnp.mean(y, axis=-1, keepdims=True)
    yc = y - mu
    var = jnp.mean(yc * yc, axis=-1, keepdims=True)
    z = yc * lax.rsqrt(var + EPS) * lng_ref[...] + lnb_ref[...]
    o_ref[...] = (_silu(z) * _silu(gate_ref[...])).astype(o_ref.dtype)


def _conv_core(u, dw, dw_b, ln_g, ln_b, *, tt=256, cw=128):
    s = u.shape[0]
    d = dw.shape[1]
    row = pl.BlockSpec((1, d), lambda t: (0, 0))
    return pl.pallas_call(
        functools.partial(_conv_kernel, tt=tt, cw=cw),
        grid=(s // tt,),
        in_specs=[
            pl.BlockSpec((tt, d), lambda t: (t, 0)),
            pl.BlockSpec((tt, d), lambda t: (t, 1)),
            pl.BlockSpec((tt, d), lambda t: (t, 2)),
            pl.BlockSpec(dw.shape, lambda t: (0, 0)),
            row, row, row,
        ],
        out_specs=pl.BlockSpec((tt, d), lambda t: (t, 0)),
        out_shape=jax.ShapeDtypeStruct((s, d), BF16),
        scratch_shapes=[pltpu.VMEM((tt + CONV_HALO, d), F32), pltpu.VMEM((tt, d), F32)],
        compiler_params=_cparams("arbitrary"),
        name="conv_module_core",
    )(u, u, u, dw, dw_b, ln_g, ln_b)


def kernel(x, c, norm_g, ada_w, ada_b, attn_w_in, attn_w_out, attn_lam_q1, attn_lam_k1, attn_lam_q2, attn_lam_k2, attn_subln_g, hgrn_w_in, hgrn_w_out, hgrn_lb_logits, hgrn_gnorm_g, conv_w_in, conv_b_in, conv_dw, conv_dw_b, conv_ln_g, conv_ln_b, conv_w_out, conv_b_out, final_g):
    bsz, s, d = x.shape
    depth = norm_g.shape[0]
    assert bsz == 1 and c.shape == (1, d)
    xs = x.reshape(s, d)
    mods = _mods(c, ada_w, ada_b)

    slopes = (2.0 ** (-8.0 * jnp.arange(1, DA_HEADS + 1, dtype=F32) / DA_HEADS)) * LOG2E
    qk_cols = 2 * DA_HEADS * DA_QK_DIM
    attn_colscale = jnp.concatenate(
        [jnp.full((1, qk_cols), DA_QK_DIM ** -0.5 * LOG2E, F32), jnp.ones((1, 2 * qk_cols), F32)], axis=1)
    lb_all = jax.nn.softmax(hgrn_lb_logits.astype(F32), axis=0)
    lb_all = jnp.cumsum(lb_all, axis=0) - lb_all[0]

    for i in range(depth):
        shift, scale, gate = (mods[i, :, k * d:(k + 1) * d] for k in range(3))
        h = _norm(xs, norm_g[i:i + 1], scale, shift, modulate=True, out_dtype=BF16)
        kind, j = i % N_MIXERS, i // N_MIXERS
        bias_out = None
        if kind == 0:
            w_in = attn_w_in[j].astype(BF16)
            qkv = _matmul(h, w_in, col0=0, n_out=3 * qk_cols, out_dtype=BF16, colscale=attn_colscale)
            g_act = _matmul(h, w_in, col0=3 * qk_cols, n_out=DA_HEADS * DA_V_DIM, out_dtype=F32)
            lam_init = 0.8 - 0.6 * math.exp(-0.3 * i)
            lam = (jnp.exp(jnp.sum(attn_lam_q1[j] * attn_lam_k1[j]))
                   - jnp.exp(jnp.sum(attn_lam_q2[j] * attn_lam_k2[j])) + lam_init).reshape(1)
            y = _attention(qkv, g_act, slopes, lam, attn_subln_g[j:j + 1], out_scale=1.0 - lam_init)
            w_out = attn_w_out[j]
        elif kind == 1:
            u = _matmul(h, hgrn_w_in[j].astype(BF16), col0=0, n_out=hgrn_w_in.shape[2], out_dtype=F32)
            o = _hgrn_core(u, lb_all[i:i + 1])
            y = _gnorm_gate(o, u, hgrn_gnorm_g[j:j + 1])
            w_out = hgrn_w_out[j]
        else:
            u = _matmul(h, conv_w_in[j].astype(BF16), col0=0, n_out=conv_w_in.shape[2], out_dtype=F32,
                        bias=conv_b_in[j:j + 1])
            y = _conv_core(u, conv_dw[j], conv_dw_b[j:j + 1], conv_ln_g[j:j + 1], conv_ln_b[j:j + 1])
            w_out = conv_w_out[j]
            bias_out = conv_b_out[j:j + 1]
        xs = _matmul(y, w_out.astype(BF16), col0=0, n_out=d, out_dtype=F32, bias=bias_out,
                     res=xs, res_gate=gate)

    zero = jnp.zeros((1, d), F32)
    out = _norm(xs, final_g.reshape(1, d), zero, zero, modulate=False, out_dtype=F32)
    return out.reshape(bsz, s, d)
```

```python
import functools
import math

import numpy as np
import jax
import jax.numpy as jnp
from jax import lax
from jax.experimental import pallas as pl
from jax.experimental.pallas import tpu as pltpu

F32 = jnp.float32
BF16 = jnp.bfloat16

EPS = 1e-6
LOG2E = 1.4426950408889634
N_MIXERS = 3

DA_HEADS = 8
DA_QK_DIM = 128
DA_V_DIM = 256
HG_HEADS = 16
HG_DIM = 128
HG_CHUNK = 64
HG_LEVELS = (32, 16, 8)
CONV_WIDTH = 31
CONV_HALO = 32

VMEM_LIMIT_BYTES = 56 * 1024 * 1024


def _cparams(*sem):
    return pltpu.CompilerParams(dimension_semantics=sem, vmem_limit_bytes=VMEM_LIMIT_BYTES)


def _silu(x):
    return x * jax.nn.sigmoid(x)


def _mods_kernel(c_ref, w_ref, b_ref, o_ref, ca_ref, *, kc):
    c = c_ref[...]
    ca_ref[...] = _silu(c)
    d = c.shape[0]
    tn = w_ref.shape[2]

    def body(i, acc):
        k0 = pl.multiple_of(i * kc, kc)
        prod = w_ref[0, pl.ds(k0, kc), :] * ca_ref[pl.ds(k0, kc), :]
        return acc + prod.reshape(kc // 8, 8, tn).sum(axis=0)

    acc = lax.fori_loop(0, d // kc, body, jnp.zeros((8, tn), F32))
    o_ref[0] = jnp.sum(acc, axis=0, keepdims=True) + b_ref[0]


def _mods(c, ada_w, ada_b, *, tn=1024, kc=64):
    depth, d, n = ada_w.shape
    return pl.pallas_call(
        functools.partial(_mods_kernel, kc=kc),
        grid=(depth, n // tn),
        in_specs=[
            pl.BlockSpec((d, 1), lambda i, j: (0, 0)),
            pl.BlockSpec((1, d, tn), lambda i, j: (i, 0, j)),
            pl.BlockSpec((1, 1, tn), lambda i, j: (i, 0, j)),
        ],
        out_specs=pl.BlockSpec((1, 1, tn), lambda i, j: (i, 0, j)),
        out_shape=jax.ShapeDtypeStruct((depth, 1, n), F32),
        scratch_shapes=[pltpu.VMEM((d, 1), F32)],
        compiler_params=_cparams("arbitrary", "arbitrary"),
        name="adaln_mods",
    )(c.reshape(d, 1), ada_w, ada_b.reshape(depth, 1, n))


def _norm_kernel(x_ref, g_ref, sc_ref, sh_ref, o_ref, *, modulate):
    x = x_ref[...]
    y = x * lax.rsqrt(jnp.mean(x * x, axis=-1, keepdims=True) + EPS) * g_ref[...]
    if modulate:
        y = y * (1.0 + sc_ref[...]) + sh_ref[...]
    o_ref[...] = y.astype(o_ref.dtype)


def _norm(x, gain, scale, shift, *, modulate, out_dtype, tm=512):
    s, d = x.shape
    row = pl.BlockSpec((1, d), lambda i: (0, 0))
    return pl.pallas_call(
        functools.partial(_norm_kernel, modulate=modulate),
        grid=(s // tm,),
        in_specs=[pl.BlockSpec((tm, d), lambda i: (i, 0)), row, row, row],
        out_specs=pl.BlockSpec((tm, d), lambda i: (i, 0)),
        out_shape=jax.ShapeDtypeStruct((s, d), out_dtype),
        compiler_params=_cparams("arbitrary"),
        name="rmsnorm_mod",
    )(x, gain, scale, shift)


def _mm_kernel(*refs, has_bias, has_colscale, has_res):
    a_ref, w_ref = refs[0], refs[1]
    o_ref = refs[-1]
    pos = 2
    acc = jnp.dot(a_ref[...], w_ref[...], preferred_element_type=F32)
    if has_bias:
        acc = acc + refs[pos][...]
        pos += 1
    if has_colscale:
        acc = acc * refs[pos][...]
        pos += 1
    if has_res:
        acc = refs[pos][...] + refs[pos + 1][...] * acc
    o_ref[...] = acc.astype(o_ref.dtype)


def _matmul(a, w, *, col0, n_out, out_dtype, bias=None, colscale=None, res=None, res_gate=None,
            tm=512, tn=1024):
    m, k = a.shape
    off = col0 // tn
    in_specs = [
        pl.BlockSpec((tm, k), lambda j, i: (i, 0)),
        pl.BlockSpec((k, tn), lambda j, i: (0, j + off)),
    ]
    args = [a, w]
    rowspec = pl.BlockSpec((1, tn), lambda j, i: (0, j))
    if bias is not None:
        in_specs.append(rowspec)
        args.append(bias)
    if colscale is not None:
        in_specs.append(rowspec)
        args.append(colscale)
    if res is not None:
        in_specs += [pl.BlockSpec((tm, tn), lambda j, i: (i, j)), rowspec]
        args += [res, res_gate]
    return pl.pallas_call(
        functools.partial(_mm_kernel, has_bias=bias is not None, has_colscale=colscale is not None,
                          has_res=res is not None),
        grid=(n_out // tn, m // tm),
        in_specs=in_specs,
        out_specs=pl.BlockSpec((tm, tn), lambda j, i: (i, j)),
        out_shape=jax.ShapeDtypeStruct((m, n_out), out_dtype),
        compiler_params=_cparams("arbitrary", "arbitrary"),
        name="matmul",
    )(*args)


def _attn_kernel(slope_ref, lam_ref, q_ref, k_ref, v_ref, g_ref, sg_ref, o_ref, acc_ref, l_ref, s_ref,
                 *, tq, out_scale):
    h = pl.program_id(0)
    i = pl.program_id(1)
    tk = tq
    dk = DA_QK_DIM
    slope = slope_ref[h]
    lam = lam_ref[0]
    acc_ref[...] = jnp.zeros_like(acc_ref)
    l_ref[...] = jnp.zeros_like(l_ref)
    colf = lax.broadcasted_iota(jnp.int32, (1, tk), 1).astype(F32)
    lanes = 128

    def scores(mi, j, m_old, masked):
        k0 = pl.multiple_of(j * tk, tk)
        bias = slope * (colf + ((j - i) * tk).astype(F32))
        qm = q_ref[:, mi * dk:(mi + 1) * dk]
        km = k_ref[pl.ds(k0, tk), mi * dk:(mi + 1) * dk]
        s = lax.dot_general(qm, km, (((1,), (1,)), ((), ())), preferred_element_type=F32) + bias
        if masked:
            r = lax.broadcasted_iota(jnp.int32, (tq, tk), 0)
            c = lax.broadcasted_iota(jnp.int32, (tq, tk), 1)
            s = jnp.where(r >= c, s, -jnp.inf)
        s_ref[mi] = s
        return jnp.maximum(m_old, jnp.max(s, axis=-1, keepdims=True))

    def accumulate(mi, j, m_old, m_new):
        vb = v_ref[pl.ds(pl.multiple_of(j * tk, tk), tk), :]
        alpha = jnp.exp2(m_old - m_new)
        p = jnp.exp2(s_ref[mi] - m_new)
        psum = p[:, 0:lanes]
        for t in range(1, tk // lanes):
            psum = psum + p[:, t * lanes:(t + 1) * lanes]
        l_ref[mi] = alpha * l_ref[mi] + psum
        acc_ref[mi] = alpha * acc_ref[mi] + jnp.dot(p.astype(BF16), vb, preferred_element_type=F32)

    def tile(j, m, masked):
        m_new = tuple(scores(mi, j, m[mi], masked) for mi in range(2))
        for mi in range(2):
            accumulate(mi, j, m[mi], m_new[mi])
        return m_new

    neg = jnp.full((tq, 1), -1e30, F32)
    m = tile(i, (neg, neg), True)
    lax.fori_loop(0, i, lambda j, m: tile(j, m, False), m)

    l1 = jnp.sum(l_ref[0], axis=-1, keepdims=True)
    l2 = jnp.sum(l_ref[1], axis=-1, keepdims=True)
    o = acc_ref[0] / l1 - lam * (acc_ref[1] / l2)
    y = o * lax.rsqrt(jnp.mean(o * o, axis=-1, keepdims=True) + EPS) * sg_ref[...] * out_scale
    o_ref[...] = (y * _silu(g_ref[...])).astype(o_ref.dtype)


def _attention(qkv, gate, slopes, lam, subln_g, *, out_scale, tq=512):
    s = qkv.shape[0]
    dv = DA_V_DIM
    nh = DA_HEADS
    smem = pl.BlockSpec(memory_space=pltpu.SMEM)
    return pl.pallas_call(
        functools.partial(_attn_kernel, tq=tq, out_scale=out_scale),
        grid=(nh, s // tq),
        in_specs=[
            smem, smem,
            pl.BlockSpec((tq, dv), lambda h, i: (i, h)),
            pl.BlockSpec((s, dv), lambda h, i: (0, nh + h)),
            pl.BlockSpec((s, dv), lambda h, i: (0, 2 * nh + h)),
            pl.BlockSpec((tq, dv), lambda h, i: (i, h)),
            pl.BlockSpec((1, dv), lambda h, i: (0, 0)),
        ],
        out_specs=pl.BlockSpec((tq, dv), lambda h, i: (i, h)),
        out_shape=jax.ShapeDtypeStruct((s, nh * dv), BF16),
        scratch_shapes=[pltpu.VMEM((2, tq, dv), F32), pltpu.VMEM((2, tq, 128), F32),
                        pltpu.VMEM((2, tq, tq), F32)],
        compiler_params=_cparams("arbitrary", "arbitrary"),
        name="diff_attention",
    )(slopes, lam, qkv, qkv, qkv, gate, subln_g)


def _hgrn_exponent_matrix():
    c = HG_CHUNK
    w = np.zeros((len(HG_LEVELS) + 1, c, c), np.float32)
    for r in range(c):
        w[0, r, :r + 1] = 1.0
        for li, m in enumerate(HG_LEVELS):
            mid = (r // (2 * m)) * 2 * m + m
            if r % (2 * m) >= m:
                w[li + 1, r, mid:r + 1] = 1.0
            else:
                w[li + 1, r, r + 1:mid] = 1.0
    return w.reshape(-1, c)


def _hgrn_kernel(q_ref, f_ref, i_ref, lb_ref, w_ref, o_ref, st_ref, *, tt):
    c = HG_CHUNK

    @pl.when(pl.program_id(1) == 0)
    def _():
        st_ref[...] = jnp.zeros_like(st_ref)

    lb = lb_ref[...]
    wall = w_ref[...]
    row = lax.broadcasted_iota(jnp.int32, (c, 1), 0)
    r2 = lax.broadcasted_iota(jnp.int32, (c, c), 0)
    c2 = lax.broadcasted_iota(jnp.int32, (c, c), 1)
    row8 = lax.broadcasted_iota(jnp.int32, (8, 1), 0)
    nt = (((1,), (1,)), ((), ()))
    tn = (((0,), (0,)), ((), ()))

    def chunk(ci, carry):
        r0 = pl.multiple_of(ci * c, c)
        q = _silu(q_ref[pl.ds(r0, c), :])
        v = i_ref[pl.ds(r0, c), :]
        fg = lb + (1.0 - lb) * jax.nn.sigmoid(f_ref[pl.ds(r0, c), :])
        kk = 1.0 - fg
        g = jnp.log(fg)
        g_hi = g.astype(BF16)
        g_lo = (g - g_hi.astype(F32)).astype(BF16)
        ex = (jnp.dot(wall, g_hi, preferred_element_type=F32)
              + jnp.dot(wall, g_lo, preferred_element_type=F32))
        b = ex[0:c]
        st = st_ref[...]
        vb = v.astype(BF16)

        o = lax.dot_general((q * jnp.exp(b)).astype(BF16), st.astype(BF16), nt,
                            preferred_element_type=F32)

        a = jnp.zeros((c, c), F32)
        for li, m in enumerate(HG_LEVELS):
            e = jnp.exp(ex[(li + 1) * c:(li + 2) * c])
            upper = (row % (2 * m)) >= m
            qt = jnp.where(upper, q * e, 0.0).astype(BF16)
            kt = jnp.where(upper, 0.0, kk * e).astype(BF16)
            al = lax.dot_general(qt, kt, nt, preferred_element_type=F32)
            if 2 * m != c:
                al = jnp.where((r2 // (2 * m)) == (c2 // (2 * m)), al, 0.0)
            a = a + al
        o = o + jnp.dot(a.astype(BF16), vb, preferred_element_type=F32)

        parts = []
        for blk in range(c // 8):
            sl = slice(8 * blk, 8 * blk + 8)
            bb, kb, v8, q8 = b[sl], kk[sl], v[sl], q[sl]
            od = jnp.zeros((8, HG_DIM), F32)
            for s in range(8):
                e = jnp.exp(jnp.where(row8 >= s, bb - bb[s:s + 1], -jnp.inf))
                w = jnp.sum(e * q8 * kb[s:s + 1], axis=-1, keepdims=True)
                od = od + w * v8[s:s + 1]
            parts.append(od)
        o_ref[pl.ds(r0, c), :] = o + jnp.concatenate(parts, axis=0)

        b_last = b[c - 1:c]
        kd = (kk * jnp.exp(b_last - b)).astype(BF16)
        st_ref[...] = jnp.exp(b_last) * st + lax.dot_general(vb, kd, tn, preferred_element_type=F32)
        return carry

    lax.fori_loop(0, tt // c, chunk, 0, unroll=True)


def _hgrn_core(u, lb, *, tt=512):
    s = u.shape[0]
    nh, dk = HG_HEADS, HG_DIM
    wall = jnp.asarray(_hgrn_exponent_matrix(), BF16)
    return pl.pallas_call(
        functools.partial(_hgrn_kernel, tt=tt),
        grid=(nh, s // tt),
        in_specs=[
            pl.BlockSpec((tt, dk), lambda h, t: (t, h)),
            pl.BlockSpec((tt, dk), lambda h, t: (t, nh + h)),
            pl.BlockSpec((tt, dk), lambda h, t: (t, 2 * nh + h)),
            pl.BlockSpec((1, dk), lambda h, t: (0, h)),
            pl.BlockSpec(wall.shape, lambda h, t: (0, 0)),
        ],
        out_specs=pl.BlockSpec((tt, dk), lambda h, t: (t, h)),
        out_shape=jax.ShapeDtypeStruct((s, nh * dk), F32),
        scratch_shapes=[pltpu.VMEM((dk, dk), F32)],
        compiler_params=_cparams("arbitrary", "arbitrary"),
        name="hgrn2_core",
    )(u, u, u, lb, wall)


def _gnorm_kernel(o_ref, gate_ref, g_ref, out_ref):
    o = o_ref[...]
    y = o * lax.rsqrt(jnp.mean(o * o, axis=-1, keepdims=True) + EPS) * g_ref[...]
    out_ref[...] = (y * _silu(gate_ref[...])).astype(out_ref.dtype)


def _gnorm_gate(o, u, gain, *, tm=512):
    s, d = o.shape
    gate_blk = (u.shape[1] - d) // d
    return pl.pallas_call(
        _gnorm_kernel,
        grid=(s // tm,),
        in_specs=[
            pl.BlockSpec((tm, d), lambda i: (i, 0)),
            pl.BlockSpec((tm, d), lambda i: (i, gate_blk)),
            pl.BlockSpec((1, d), lambda i: (0, 0)),
        ],
        out_specs=pl.BlockSpec((tm, d), lambda i: (i, 0)),
        out_shape=jax.ShapeDtypeStruct((s, d), BF16),
        compiler_params=_cparams("arbitrary"),
        name="hgrn2_gnorm_gate",
    )(o, u, gain)


def _conv_kernel(a_ref, ag_ref, gate_ref, dw_ref, dwb_ref, lng_ref, lnb_ref, o_ref, ybuf, cbuf,
                 *, tt, cw):
    halo = CONV_HALO
    d = a_ref.shape[1]

    @pl.when(pl.program_id(0) == 0)
    def _():
        ybuf[0:halo, :] = jnp.zeros((halo, d), F32)

    @pl.when(pl.program_id(0) > 0)
    def _():
        ybuf[0:halo, :] = ybuf[tt:tt + halo, :]

    ybuf[halo:halo + tt, :] = a_ref[...] * jax.nn.sigmoid(ag_ref[...])
    first = halo - (CONV_WIDTH - 1)

    def lanes(ci, carry):
        c0 = pl.multiple_of(ci * cw, cw)
        acc = jnp.broadcast_to(dwb_ref[:, pl.ds(c0, cw)], (tt, cw))
        for k in range(CONV_WIDTH):
            acc = acc + ybuf[first + k:first + k + tt, pl.ds(c0, cw)] * dw_ref[k:k + 1, pl.ds(c0, cw)]
        cbuf[:, pl.ds(c0, cw)] = acc
        return carry

    lax.fori_loop(0, d // cw, lanes, 0)

    y = cbuf[...]
    mu = jnp.mean(y, axis=-1, keepdims=True)
    yc = y - mu
    var = jnp.mean(yc * yc, axis=-1, keepdims=True)
    z = yc * lax.rsqrt(var + EPS) * lng_ref[...] + lnb_ref[...]
    o_ref[...] = (_silu(z) * _silu(gate_ref[...])).astype(o_ref.dtype)


def _conv_core(u, dw, dw_b, ln_g, ln_b, *, tt=256, cw=128):
    s = u.shape[0]
    d = dw.shape[1]
    row = pl.BlockSpec((1, d), lambda t: (0, 0))
    return pl.pallas_call(
        functools.partial(_conv_kernel, tt=tt, cw=cw),
        grid=(s // tt,),
        in_specs=[
            pl.BlockSpec((tt, d), lambda t: (t, 0)),
            pl.BlockSpec((tt, d), lambda t: (t, 1)),
            pl.BlockSpec((tt, d), lambda t: (t, 2)),
            pl.BlockSpec(dw.shape, lambda t: (0, 0)),
            row, row, row,
        ],
        out_specs=pl.BlockSpec((tt, d), lambda t: (t, 0)),
        out_shape=jax.ShapeDtypeStruct((s, d), BF16),
        scratch_shapes=[pltpu.VMEM((tt + CONV_HALO, d), F32), pltpu.VMEM((tt, d), F32)],
        compiler_params=_cparams("arbitrary"),
        name="conv_module_core",
    )(u, u, u, dw, dw_b, ln_g, ln_b)


def kernel(x, c, norm_g, ada_w, ada_b, attn_w_in, attn_w_out, attn_lam_q1, attn_lam_k1, attn_lam_q2, attn_lam_k2, attn_subln_g, hgrn_w_in, hgrn_w_out, hgrn_lb_logits, hgrn_gnorm_g, conv_w_in, conv_b_in, conv_dw, conv_dw_b, conv_ln_g, conv_ln_b, conv_w_out, conv_b_out, final_g):
    bsz, s, d = x.shape
    depth = norm_g.shape[0]
    assert bsz == 1 and c.shape == (1, d)
    xs = x.reshape(s, d)
    mods = _mods(c, ada_w, ada_b)

    slopes = (2.0 ** (-8.0 * jnp.arange(1, DA_HEADS + 1, dtype=F32) / DA_HEADS)) * LOG2E
    qk_cols = 2 * DA_HEADS * DA_QK_DIM
    attn_colscale = jnp.concatenate(
        [jnp.full((1, qk_cols), DA_QK_DIM ** -0.5 * LOG2E, F32), jnp.ones((1, 2 * qk_cols), F32)], axis=1)
    lb_all = jax.nn.softmax(hgrn_lb_logits.astype(F32), axis=0)
    lb_all = jnp.cumsum(lb_all, axis=0) - lb_all[0]

    for i in range(depth):
        shift, scale, gate = (mods[i, :, k * d:(k + 1) * d] for k in range(3))
        h = _norm(xs, norm_g[i:i + 1], scale, shift, modulate=True, out_dtype=BF16)
        kind, j = i % N_MIXERS, i // N_MIXERS
        bias_out = None
        if kind == 0:
            w_in = attn_w_in[j].astype(BF16)
            qkv = _matmul(h, w_in, col0=0, n_out=3 * qk_cols, out_dtype=BF16, colscale=attn_colscale)
            g_act = _matmul(h, w_in, col0=3 * qk_cols, n_out=DA_HEADS * DA_V_DIM, out_dtype=F32)
            lam_init = 0.8 - 0.6 * math.exp(-0.3 * i)
            lam = (jnp.exp(jnp.sum(attn_lam_q1[j] * attn_lam_k1[j]))
                   - jnp.exp(jnp.sum(attn_lam_q2[j] * attn_lam_k2[j])) + lam_init).reshape(1)
            y = _attention(qkv, g_act, slopes, lam, attn_subln_g[j:j + 1], out_scale=1.0 - lam_init)
            w_out = attn_w_out[j]
        elif kind == 1:
            u = _matmul(h, hgrn_w_in[j].astype(BF16), col0=0, n_out=hgrn_w_in.shape[2], out_dtype=F32)
            o = _hgrn_core(u, lb_all[i:i + 1])
            y = _gnorm_gate(o, u, hgrn_gnorm_g[j:j + 1])
            w_out = hgrn_w_out[j]
        else:
            u = _matmul(h, conv_w_in[j].astype(BF16), col0=0, n_out=conv_w_in.shape[2], out_dtype=F32,
                        bias=conv_b_in[j:j + 1])
            y = _conv_core(u, conv_dw[j], conv_dw_b[j:j + 1], conv_ln_g[j:j + 1], conv_ln_b[j:j + 1])
            w_out = conv_w_out[j]
            bias_out = conv_b_out[j:j + 1]
        xs = _matmul(y, w_out.astype(BF16), col0=0, n_out=d, out_dtype=F32, bias=bias_out,
                     res=xs, res_gate=gate)

    zero = jnp.zeros((1, d), F32)
    out = _norm(xs, final_g.reshape(1, d), zero, zero, modulate=False, out_dtype=F32)
    return out.reshape(bsz, s, d)
```

```python
import functools
import math

import numpy as np
import jax
import jax.numpy as jnp
from jax import lax
from jax.experimental import pallas as pl
from jax.experimental.pallas import tpu as pltpu

F32 = jnp.float32
BF16 = jnp.bfloat16

EPS = 1e-6
LOG2E = 1.4426950408889634
N_MIXERS = 3

DA_HEADS = 8
DA_QK_DIM = 128
DA_V_DIM = 256
HG_HEADS = 16
HG_DIM = 128
HG_CHUNK = 64
HG_LEVELS = (32, 16, 8)
CONV_WIDTH = 31
CONV_HALO = 32

VMEM_LIMIT_BYTES = 56 * 1024 * 1024


def _cparams(*sem):
    return pltpu.CompilerParams(dimension_semantics=sem, vmem_limit_bytes=VMEM_LIMIT_BYTES)


def _sigmoid(x):
    return 0.5 * jnp.tanh(0.5 * x) + 0.5


def _silu(x):
    return x * _sigmoid(x)


def _mods_kernel(c_ref, w_ref, b_ref, o_ref, ca_ref, *, kc):
    c = c_ref[...]
    ca_ref[...] = _silu(c)
    d = c.shape[0]
    tn = w_ref.shape[2]

    def body(i, acc):
        k0 = pl.multiple_of(i * kc, kc)
        prod = w_ref[0, pl.ds(k0, kc), :] * ca_ref[pl.ds(k0, kc), :]
        return acc + prod.reshape(kc // 8, 8, tn).sum(axis=0)

    acc = lax.fori_loop(0, d // kc, body, jnp.zeros((8, tn), F32))
    o_ref[0] = jnp.sum(acc, axis=0, keepdims=True) + b_ref[0]


def _mods(c, ada_w, ada_b, *, tn=1024, kc=64):
    depth, d, n = ada_w.shape
    return pl.pallas_call(
        functools.partial(_mods_kernel, kc=kc),
        grid=(depth, n // tn),
        in_specs=[
            pl.BlockSpec((d, 1), lambda i, j: (0, 0)),
            pl.BlockSpec((1, d, tn), lambda i, j: (i, 0, j)),
            pl.BlockSpec((1, 1, tn), lambda i, j: (i, 0, j)),
        ],
        out_specs=pl.BlockSpec((1, 1, tn), lambda i, j: (i, 0, j)),
        out_shape=jax.ShapeDtypeStruct((depth, 1, n), F32),
        scratch_shapes=[pltpu.VMEM((d, 1), F32)],
        compiler_params=_cparams("arbitrary", "arbitrary"),
        name="adaln_mods",
    )(c.reshape(d, 1), ada_w, ada_b.reshape(depth, 1, n))


def _norm_kernel(x_ref, g_ref, sc_ref, sh_ref, o_ref, *, modulate):
    x = x_ref[...]
    y = x * lax.rsqrt(jnp.mean(x * x, axis=-1, keepdims=True) + EPS) * g_ref[...]
    if modulate:
        y = y * (1.0 + sc_ref[...]) + sh_ref[...]
    o_ref[...] = y.astype(o_ref.dtype)


def _norm(x, gain, scale, shift, *, modulate, out_dtype, tm=512):
    s, d = x.shape
    row = pl.BlockSpec((1, d), lambda i: (0, 0))
    return pl.pallas_call(
        functools.partial(_norm_kernel, modulate=modulate),
        grid=(s // tm,),
        in_specs=[pl.BlockSpec((tm, d), lambda i: (i, 0)), row, row, row],
        out_specs=pl.BlockSpec((tm, d), lambda i: (i, 0)),
        out_shape=jax.ShapeDtypeStruct((s, d), out_dtype),
        compiler_params=_cparams("arbitrary"),
        name="rmsnorm_mod",
    )(x, gain, scale, shift)


def _mm_kernel(*refs, has_bias, has_colscale):
    a_ref, w_ref = refs[0], refs[1]
    o_ref, wb_ref = refs[-2], refs[-1]
    pos = 2

    @pl.when(pl.program_id(1) == 0)
    def _():
        wb_ref[...] = w_ref[...].astype(BF16)

    acc = jnp.dot(a_ref[...], wb_ref[...], preferred_element_type=F32)
    if has_bias:
        acc = acc + refs[pos][...]
        pos += 1
    if has_colscale:
        acc = acc * refs[pos][...]
    o_ref[...] = acc.astype(o_ref.dtype)


def _matmul(a, w, layer, *, out_dtype, bias=None, colscale=None, tm=512, tn=1024):
    m, k = a.shape
    n_out = w.shape[2]
    in_specs = [
        pl.BlockSpec((tm, k), lambda j, i: (i, 0)),
        pl.BlockSpec((None, k, tn), lambda j, i: (layer, 0, j)),
    ]
    args = [a, w]
    rowspec = pl.BlockSpec((1, tn), lambda j, i: (0, j))
    for extra in (bias, colscale):
        if extra is not None:
            in_specs.append(rowspec)
            args.append(extra)
    return pl.pallas_call(
        functools.partial(_mm_kernel, has_bias=bias is not None, has_colscale=colscale is not None),
        grid=(n_out // tn, m // tm),
        in_specs=in_specs,
        out_specs=pl.BlockSpec((tm, tn), lambda j, i: (i, j)),
        out_shape=jax.ShapeDtypeStruct((m, n_out), out_dtype),
        scratch_shapes=[pltpu.VMEM((k, tn), BF16)],
        compiler_params=_cparams("arbitrary", "arbitrary"),
        name="matmul_in",
    )(*args)


def _outproj_kernel(*refs, has_bias, modulate, write_x):
    a_ref, w_ref, x_ref, gate_ref = refs[:4]
    pos = 4
    y = jnp.dot(a_ref[...], w_ref[...], preferred_element_type=F32)
    if has_bias:
        y = y + refs[pos][...]
        pos += 1
    g_ref, sc_ref, sh_ref = refs[pos:pos + 3]
    xn = x_ref[...] + gate_ref[...] * y
    if write_x:
        refs[-2][...] = xn
    hn = xn * lax.rsqrt(jnp.mean(xn * xn, axis=-1, keepdims=True) + EPS) * g_ref[...]
    if modulate:
        hn = hn * (1.0 + sc_ref[...]) + sh_ref[...]
    refs[-1][...] = hn.astype(refs[-1].dtype)


def _outproj(a, w, x, gate, bias, gain, scale, shift, *, modulate, write_x, out_dtype, tm=256):
    m, k = a.shape
    d = w.shape[1]
    row = pl.BlockSpec((1, d), lambda i: (0, 0))
    tile = pl.BlockSpec((tm, d), lambda i: (i, 0))
    in_specs = [pl.BlockSpec((tm, k), lambda i: (i, 0)), pl.BlockSpec((k, d), lambda i: (0, 0)), tile, row]
    args = [a, w, x, gate]
    if bias is not None:
        in_specs.append(row)
        args.append(bias)
    in_specs += [row, row, row]
    args += [gain, scale, shift]
    out_specs = [tile]
    out_shape = [jax.ShapeDtypeStruct((m, d), out_dtype)]
    if write_x:
        out_specs = [tile, tile]
        out_shape = [jax.ShapeDtypeStruct((m, d), F32)] + out_shape
    return pl.pallas_call(
        functools.partial(_outproj_kernel, has_bias=bias is not None, modulate=modulate, write_x=write_x),
        grid=(m // tm,),
        in_specs=in_specs,
        out_specs=out_specs,
        out_shape=out_shape,
        compiler_params=_cparams("arbitrary"),
        name="matmul_out",
    )(*args)


def _attn_kernel(slope_ref, lam_ref, q_ref, k_ref, v_ref, g_ref, sg_ref, o_ref, acc_ref, l_ref, s_ref,
                 *, tq, out_scale):
    h = pl.program_id(0)
    i = pl.program_id(1)
    dk = DA_QK_DIM
    slope = slope_ref[h]
    lam = lam_ref[0]
    acc_ref[...] = jnp.zeros_like(acc_ref)
    l_ref[...] = jnp.zeros_like(l_ref)
    lanes = 128
    q0 = i * tq

    def scores(mi, k0, tk, m_old, masked):
        colf = lax.broadcasted_iota(jnp.int32, (1, tk), 1).astype(F32)
        bias = slope * (colf + (k0 - q0).astype(F32))
        qm = q_ref[:, mi * dk:(mi + 1) * dk]
        km = k_ref[pl.ds(k0, tk), mi * dk:(mi + 1) * dk]
        s = lax.dot_general(qm, km, (((1,), (1,)), ((), ())), preferred_element_type=F32) + bias
        if masked:
            r = lax.broadcasted_iota(jnp.int32, (tq, tk), 0)
            c = lax.broadcasted_iota(jnp.int32, (tq, tk), 1)
            s = jnp.where(r >= c, s, -jnp.inf)
        s_ref[mi, :, 0:tk] = s
        return jnp.maximum(m_old, jnp.max(s, axis=-1, keepdims=True))

    def accumulate(mi, k0, tk, m_old, m_new):
        vb = v_ref[pl.ds(k0, tk), :]
        alpha = jnp.exp2(m_old - m_new)
        p = jnp.exp2(s_ref[mi, :, 0:tk] - m_new)
        psum = p[:, 0:lanes]
        for t in range(1, tk // lanes):
            psum = psum + p[:, t * lanes:(t + 1) * lanes]
        l_ref[mi] = alpha * l_ref[mi] + psum
        acc_ref[mi] = alpha * acc_ref[mi] + jnp.dot(p.astype(BF16), vb, preferred_element_type=F32)

    def tile(k0, tk, m, masked):
        k0 = pl.multiple_of(k0, tq)
        m_new = tuple(scores(mi, k0, tk, m[mi], masked) for mi in range(2))
        for mi in range(2):
            accumulate(mi, k0, tk, m[mi], m_new[mi])
        return m_new

    neg = jnp.full((tq, 1), -1e30, F32)
    m = tile(q0, tq, (neg, neg), True)
    wide = s_ref.shape[2]
    w = tq
    end = q0
    while w < wide:
        take = (i // (w // tq)) % 2 == 1
        m = lax.cond(take, functools.partial(lambda m, w, end: tile(end - w, w, m, False), w=w, end=end),
                     lambda m: m, m)
        end = end - jnp.where(take, w, 0)
        w *= 2
    lax.fori_loop(0, i // (wide // tq), lambda jj, m: tile(jj * wide, wide, m, False), m)

    l1 = jnp.sum(l_ref[0], axis=-1, keepdims=True)
    l2 = jnp.sum(l_ref[1], axis=-1, keepdims=True)
    o = acc_ref[0] / l1 - lam * (acc_ref[1] / l2)
    y = o * lax.rsqrt(jnp.mean(o * o, axis=-1, keepdims=True) + EPS) * sg_ref[...] * out_scale
    o_ref[...] = (y * _silu(g_ref[...].astype(F32))).astype(o_ref.dtype)


def _attention(u, slopes, lam, subln_g, *, out_scale, tq=512, tk_wide=2048):
    s = u.shape[0]
    dv = DA_V_DIM
    nh = DA_HEADS
    smem = pl.BlockSpec(memory_space=pltpu.SMEM)
    return pl.pallas_call(
        functools.partial(_attn_kernel, tq=tq, out_scale=out_scale),
        grid=(nh, s // tq),
        in_specs=[
            smem, smem,
            pl.BlockSpec((tq, dv), lambda h, i: (i, h)),
            pl.BlockSpec((s, dv), lambda h, i: (0, nh + h)),
            pl.BlockSpec((s, dv), lambda h, i: (0, 2 * nh + h)),
            pl.BlockSpec((tq, dv), lambda h, i: (i, 3 * nh + h)),
            pl.BlockSpec((1, dv), lambda h, i: (0, 0)),
        ],
        out_specs=pl.BlockSpec((tq, dv), lambda h, i: (i, h)),
        out_shape=jax.ShapeDtypeStruct((s, nh * dv), BF16),
        scratch_shapes=[pltpu.VMEM((2, tq, dv), F32), pltpu.VMEM((2, tq, 128), F32),
                        pltpu.VMEM((2, tq, tk_wide), F32)],
        compiler_params=_cparams("arbitrary", "arbitrary"),
        name="diff_attention",
    )(slopes, lam, u, u, u, u, subln_g)


def _hgrn_exponent_matrix():
    c = HG_CHUNK
    w = np.zeros((len(HG_LEVELS) + 1, c, c), np.float32)
    for r in range(c):
        w[0, r, :r + 1] = 1.0
        for li, m in enumerate(HG_LEVELS):
            mid = (r // (2 * m)) * 2 * m + m
            if r % (2 * m) >= m:
                w[li + 1, r, mid:r + 1] = 1.0
            else:
                w[li + 1, r, r + 1:mid] = 1.0
    return w.reshape(-1, c)


def _hgrn_kernel(q_ref, f_ref, i_ref, lb_ref, w_ref, o_ref, st_ref, *, tt):
    c = HG_CHUNK

    @pl.when(pl.program_id(1) == 0)
    def _():
        st_ref[...] = jnp.zeros_like(st_ref)

    lb = lb_ref[...]
    wall = w_ref[...]
    row = lax.broadcasted_iota(jnp.int32, (c, 1), 0)
    r2 = lax.broadcasted_iota(jnp.int32, (c, c), 0)
    c2 = lax.broadcasted_iota(jnp.int32, (c, c), 1)
    row8 = lax.broadcasted_iota(jnp.int32, (8, 1), 0)
    nt = (((1,), (1,)), ((), ()))
    tn = (((0,), (0,)), ((), ()))

    def chunk(ci, carry):
        r0 = pl.multiple_of(ci * c, c)
        q = _silu(q_ref[pl.ds(r0, c), :])
        v = i_ref[pl.ds(r0, c), :]
        fg = lb + (1.0 - lb) * jax.nn.sigmoid(f_ref[pl.ds(r0, c), :])
        kk = 1.0 - fg
        g = jnp.log(fg)
        g_hi = g.astype(BF16)
        g_lo = (g - g_hi.astype(F32)).astype(BF16)
        ex = (jnp.dot(wall, g_hi, preferred_element_type=F32)
              + jnp.dot(wall, g_lo, preferred_element_type=F32))
        b = ex[0:c]
        st = st_ref[...]
        vb = v.astype(BF16)

        o = lax.dot_general((q * jnp.exp(b)).astype(BF16), st.astype(BF16), nt,
                            preferred_element_type=F32)

        a = jnp.zeros((c, c), F32)
        for li, m in enumerate(HG_LEVELS):
            e = jnp.exp(ex[(li + 1) * c:(li + 2) * c])
            upper = (row % (2 * m)) >= m
            qt = jnp.where(upper, q * e, 0.0).astype(BF16)
            kt = jnp.where(upper, 0.0, kk * e).astype(BF16)
            al = lax.dot_general(qt, kt, nt, preferred_element_type=F32)
            if 2 * m != c:
                al = jnp.where((r2 // (2 * m)) == (c2 // (2 * m)), al, 0.0)
            a = a + al
        o = o + jnp.dot(a.astype(BF16), vb, preferred_element_type=F32)

        parts = []
        for blk in range(c // 8):
            sl = slice(8 * blk, 8 * blk + 8)
            bb, kb, v8, q8 = b[sl], kk[sl], v[sl], q[sl]
            od = jnp.zeros((8, HG_DIM), F32)
            for s in range(8):
                e = jnp.exp(jnp.where(row8 >= s, bb - bb[s:s + 1], -jnp.inf))
                w = jnp.sum(e * q8 * kb[s:s + 1], axis=-1, keepdims=True)
                od = od + w * v8[s:s + 1]
            parts.append(od)
        o_ref[pl.ds(r0, c), :] = o + jnp.concatenate(parts, axis=0)

        b_last = b[c - 1:c]
        kd = (kk * jnp.exp(b_last - b)).astype(BF16)
        st_ref[...] = jnp.exp(b_last) * st + lax.dot_general(vb, kd, tn, preferred_element_type=F32)
        return carry

    lax.fori_loop(0, tt // c, chunk, 0, unroll=True)


def _hgrn_core(u, lb, *, tt=512):
    s = u.shape[0]
    nh, dk = HG_HEADS, HG_DIM
    wall = jnp.asarray(_hgrn_exponent_matrix(), BF16)
    return pl.pallas_call(
        functools.partial(_hgrn_kernel, tt=tt),
        grid=(nh, s // tt),
        in_specs=[
            pl.BlockSpec((tt, dk), lambda h, t: (t, h)),
            pl.BlockSpec((tt, dk), lambda h, t: (t, nh + h)),
            pl.BlockSpec((tt, dk), lambda h, t: (t, 2 * nh + h)),
            pl.BlockSpec((1, dk), lambda h, t: (0, h)),
            pl.BlockSpec(wall.shape, lambda h, t: (0, 0)),
        ],
        out_specs=pl.BlockSpec((tt, dk), lambda h, t: (t, h)),
        out_shape=jax.ShapeDtypeStruct((s, nh * dk), F32),
        scratch_shapes=[pltpu.VMEM((dk, dk), F32)],
        compiler_params=_cparams("arbitrary", "arbitrary"),
        name="hgrn2_core",
    )(u, u, u, lb, wall)


def _gnorm_kernel(o_ref, gate_ref, g_ref, out_ref):
    o = o_ref[...]
    y = o * lax.rsqrt(jnp.mean(o * o, axis=-1, keepdims=True) + EPS) * g_ref[...]
    out_ref[...] = (y * _silu(gate_ref[...])).astype(out_ref.dtype)


def _gnorm_gate(o, u, gain, *, tm=512):
    s, d = o.shape
    gate_blk = (u.shape[1] - d) // d
    return pl.pallas_call(
        _gnorm_kernel,
        grid=(s // tm,),
        in_specs=[
            pl.BlockSpec((tm, d), lambda i: (i, 0)),
            pl.BlockSpec((tm, d), lambda i: (i, gate_blk)),
            pl.BlockSpec((1, d), lambda i: (0, 0)),
        ],
        out_specs=pl.BlockSpec((tm, d), lambda i: (i, 0)),
        out_shape=jax.ShapeDtypeStruct((s, d), BF16),
        compiler_params=_cparams("arbitrary"),
        name="hgrn2_gnorm_gate",
    )(o, u, gain)


def _conv_kernel(a_ref, ag_ref, gate_ref, dw_ref, dwb_ref, lng_ref, lnb_ref, o_ref, ybuf, cbuf, shbuf,
                 *, tt, cw):
    halo = CONV_HALO
    d = a_ref.shape[1]

    @pl.when(pl.program_id(0) == 0)
    def _():
        ybuf[0:halo, :] = jnp.zeros((halo, d), F32)

    @pl.when(pl.program_id(0) > 0)
    def _():
        ybuf[0:halo, :] = ybuf[tt:tt + halo, :]

    strip = 16

    def glu(si, carry):
        r = pl.multiple_of(si * strip, strip)
        ybuf[pl.ds(pl.multiple_of(halo + r, strip), strip), :] = (
            a_ref[pl.ds(r, strip), :] * _sigmoid(ag_ref[pl.ds(r, strip), :]))
        return carry

    lax.fori_loop(0, tt // strip, glu, 0)
    first = halo - (CONV_WIDTH - 1)

    def lanes(ci, carry):
        c0 = pl.multiple_of(ci * cw, cw)
        acc = jnp.broadcast_to(dwb_ref[:, pl.ds(c0, cw)], (tt, cw))
        for rho in range(8):
            taps = [k for k in range(CONV_WIDTH) if (first + k) % 8 == rho]
            if not taps:
                continue
            span = tt + 8 * max((first + k) // 8 for k in taps)
            if rho:
                shbuf[rho, 0:span, :] = ybuf[rho:rho + span, pl.ds(c0, cw)]
            for k in taps:
                a8 = 8 * ((first + k) // 8)
                yk = shbuf[rho, a8:a8 + tt, :] if rho else ybuf[a8:a8 + tt, pl.ds(c0, cw)]
                acc = acc + yk * dw_ref[k:k + 1, pl.ds(c0, cw)]
        cbuf[:, pl.ds(c0, cw)] = acc
        return carry

    lax.fori_loop(0, d // cw, lanes, 0)

    def layernorm_gate(si, carry):
        r = pl.multiple_of(si * strip, strip)
        y = cbuf[pl.ds(r, strip), :]
        mu = jnp.mean(y, axis=-1, keepdims=True)
        yc = y - mu
        var = jnp.mean(yc * yc, axis=-1, keepdims=True)
        z = yc * lax.rsqrt(var + EPS) * lng_ref[...] + lnb_ref[...]
        o_ref[pl.ds(r, strip), :] = (_silu(z) * _silu(gate_ref[pl.ds(r, strip), :])).astype(o_ref.dtype)
        return carry

    lax.fori_loop(0, tt // strip, layernorm_gate, 0, unroll=2)


def _conv_core(u, dw, dw_b, ln_g, ln_b, *, tt=256, cw=128):
    s = u.shape[0]
    d = dw.shape[1]
    row = pl.BlockSpec((1, d), lambda t: (0, 0))
    return pl.pallas_call(
        functools.partial(_conv_kernel, tt=tt, cw=cw),
        grid=(s // tt,),
        in_specs=[
            pl.BlockSpec((tt, d), lambda t: (t, 0)),
            pl.BlockSpec((tt, d), lambda t: (t, 1)),
            pl.BlockSpec((tt, d), lambda t: (t, 2)),
            pl.BlockSpec(dw.shape, lambda t: (0, 0)),
            row, row, row,
        ],
        out_specs=pl.BlockSpec((tt, d), lambda t: (t, 0)),
        out_shape=jax.ShapeDtypeStruct((s, d), BF16),
        scratch_shapes=[pltpu.VMEM((tt + CONV_HALO, d), F32), pltpu.VMEM((tt, d), F32),
                        pltpu.VMEM((8, tt + CONV_HALO, cw), F32)],
        compiler_params=_cparams("arbitrary"),
        name="conv_module_core",
    )(u, u, u, dw, dw_b, ln_g, ln_b)


def kernel(x, c, norm_g, ada_w, ada_b, attn_w_in, attn_w_out, attn_lam_q1, attn_lam_k1, attn_lam_q2, attn_lam_k2, attn_subln_g, hgrn_w_in, hgrn_w_out, hgrn_lb_logits, hgrn_gnorm_g, conv_w_in, conv_b_in, conv_dw, conv_dw_b, conv_ln_g, conv_ln_b, conv_w_out, conv_b_out, final_g):
    bsz, s, d = x.shape
    depth = norm_g.shape[0]
    assert bsz == 1 and c.shape == (1, d)
    xs = x.reshape(s, d)
    mods = _mods(c, ada_w, ada_b)

    slopes = (2.0 ** (-8.0 * jnp.arange(1, DA_HEADS + 1, dtype=F32) / DA_HEADS)) * LOG2E
    qk_cols = 2 * DA_HEADS * DA_QK_DIM
    attn_colscale = jnp.concatenate(
        [jnp.full((1, qk_cols), DA_QK_DIM ** -0.5 * LOG2E, F32), jnp.ones((1, 3 * qk_cols), F32)], axis=1)
    lb_all = jax.nn.softmax(hgrn_lb_logits.astype(F32), axis=0)
    lb_all = jnp.cumsum(lb_all, axis=0) - lb_all[0]

    def mod_rows(i):
        return tuple(mods[i, :, k * d:(k + 1) * d] for k in range(3))

    shift, scale, gate = mod_rows(0)
    h = _norm(xs, norm_g[0:1], scale, shift, modulate=True, out_dtype=BF16)
    for i in range(depth):
        kind, j = i % N_MIXERS, i // N_MIXERS
        bias_out = None
        if kind == 0:
            u = _matmul(h, attn_w_in, j, out_dtype=BF16, colscale=attn_colscale)
            lam_init = 0.8 - 0.6 * math.exp(-0.3 * i)
            lam = (jnp.exp(jnp.sum(attn_lam_q1[j] * attn_lam_k1[j]))
                   - jnp.exp(jnp.sum(attn_lam_q2[j] * attn_lam_k2[j])) + lam_init).reshape(1)
            y = _attention(u, slopes, lam, attn_subln_g[j:j + 1], out_scale=1.0 - lam_init)
            w_out = attn_w_out[j]
        elif kind == 1:
            u = _matmul(h, hgrn_w_in, j, out_dtype=F32)
            o = _hgrn_core(u, lb_all[i:i + 1])
            y = _gnorm_gate(o, u, hgrn_gnorm_g[j:j + 1])
            w_out = hgrn_w_out[j]
        else:
            u = _matmul(h, conv_w_in, j, out_dtype=F32, bias=conv_b_in[j:j + 1])
            y = _conv_core(u, conv_dw[j], conv_dw_b[j:j + 1], conv_ln_g[j:j + 1], conv_ln_b[j:j + 1])
            w_out = conv_w_out[j]
            bias_out = conv_b_out[j:j + 1]
        if i + 1 < depth:
            shift_n, scale_n, gate_n = mod_rows(i + 1)
            xs, h = _outproj(y, w_out.astype(BF16), xs, gate, bias_out, norm_g[i + 1:i + 2], scale_n, shift_n,
                             modulate=True, write_x=True, out_dtype=BF16)
            gate = gate_n
        else:
            zero = jnp.zeros((1, d), F32)
            out, = _outproj(y, w_out.astype(BF16), xs, gate, bias_out, final_g.reshape(1, d), zero, zero,
                            modulate=False, write_x=False, out_dtype=F32)
    return out.reshape(bsz, s, d)
```

```python
import functools
import math

import numpy as np
import jax
import jax.numpy as jnp
from jax import lax
from jax.experimental import pallas as pl
from jax.experimental.pallas import tpu as pltpu

F32 = jnp.float32
BF16 = jnp.bfloat16

EPS = 1e-6
LOG2E = 1.4426950408889634
N_MIXERS = 3

DA_HEADS = 8
DA_QK_DIM = 128
DA_V_DIM = 256
HG_HEADS = 16
HG_DIM = 128
HG_CHUNK = 64
HG_LEVELS = (32, 16, 8)
HG_SAFE_EXP = 80.0
CONV_WIDTH = 31
CONV_HALO = 32

VMEM_LIMIT_BYTES = 56 * 1024 * 1024


def _cparams(*sem):
    return pltpu.CompilerParams(dimension_semantics=sem, vmem_limit_bytes=VMEM_LIMIT_BYTES)


def _sigmoid(x):
    return 0.5 * jnp.tanh(0.5 * x) + 0.5


def _silu(x):
    return x * _sigmoid(x)


def _mods_kernel(c_ref, w_ref, b_ref, o_ref, ca_ref, *, kc):
    c = c_ref[...]
    ca_ref[...] = _silu(c)
    d = c.shape[0]
    tn = w_ref.shape[2]

    def body(i, acc):
        k0 = pl.multiple_of(i * kc, kc)
        prod = w_ref[0, pl.ds(k0, kc), :] * ca_ref[pl.ds(k0, kc), :]
        return acc + prod.reshape(kc // 8, 8, tn).sum(axis=0)

    acc = lax.fori_loop(0, d // kc, body, jnp.zeros((8, tn), F32))
    o_ref[0] = jnp.sum(acc, axis=0, keepdims=True) + b_ref[0]


def _mods(c, ada_w, ada_b, *, tn=1024, kc=64):
    depth, d, n = ada_w.shape
    return pl.pallas_call(
        functools.partial(_mods_kernel, kc=kc),
        grid=(depth, n // tn),
        in_specs=[
            pl.BlockSpec((d, 1), lambda i, j: (0, 0)),
            pl.BlockSpec((1, d, tn), lambda i, j: (i, 0, j)),
            pl.BlockSpec((1, 1, tn), lambda i, j: (i, 0, j)),
        ],
        out_specs=pl.BlockSpec((1, 1, tn), lambda i, j: (i, 0, j)),
        out_shape=jax.ShapeDtypeStruct((depth, 1, n), F32),
        scratch_shapes=[pltpu.VMEM((d, 1), F32)],
        compiler_params=_cparams("arbitrary", "arbitrary"),
        name="adaln_mods",
    )(c.reshape(d, 1), ada_w, ada_b.reshape(depth, 1, n))


def _norm_kernel(x_ref, g_ref, sc_ref, sh_ref, o_ref, *, modulate):
    x = x_ref[...]
    y = x * lax.rsqrt(jnp.mean(x * x, axis=-1, keepdims=True) + EPS) * g_ref[...]
    if modulate:
        y = y * (1.0 + sc_ref[...]) + sh_ref[...]
    o_ref[...] = y.astype(o_ref.dtype)


def _norm(x, gain, scale, shift, *, modulate, out_dtype, tm=512):
    s, d = x.shape
    row = pl.BlockSpec((1, d), lambda i: (0, 0))
    return pl.pallas_call(
        functools.partial(_norm_kernel, modulate=modulate),
        grid=(s // tm,),
        in_specs=[pl.BlockSpec((tm, d), lambda i: (i, 0)), row, row, row],
        out_specs=pl.BlockSpec((tm, d), lambda i: (i, 0)),
        out_shape=jax.ShapeDtypeStruct((s, d), out_dtype),
        compiler_params=_cparams("arbitrary"),
        name="rmsnorm_mod",
    )(x, gain, scale, shift)


def _mm_kernel(*refs, has_bias, has_colscale):
    a_ref, w_ref = refs[0], refs[1]
    o_ref, wb_ref = refs[-2], refs[-1]
    pos = 2

    @pl.when(pl.program_id(1) == 0)
    def _():
        wb_ref[...] = w_ref[...].astype(BF16)

    acc = jnp.dot(a_ref[...], wb_ref[...], preferred_element_type=F32)
    if has_bias:
        acc = acc + refs[pos][...]
        pos += 1
    if has_colscale:
        acc = acc * refs[pos][...]
    o_ref[...] = acc.astype(o_ref.dtype)


def _matmul(a, w, layer, *, out_dtype, bias=None, colscale=None, tm=512, tn=1024):
    m, k = a.shape
    n_out = w.shape[2]
    in_specs = [
        pl.BlockSpec((tm, k), lambda j, i: (i, 0)),
        pl.BlockSpec((None, k, tn), lambda j, i: (layer, 0, j)),
    ]
    args = [a, w]
    rowspec = pl.BlockSpec((1, tn), lambda j, i: (0, j))
    for extra in (bias, colscale):
        if extra is not None:
            in_specs.append(rowspec)
            args.append(extra)
    return pl.pallas_call(
        functools.partial(_mm_kernel, has_bias=bias is not None, has_colscale=colscale is not None),
        grid=(n_out // tn, m // tm),
        in_specs=in_specs,
        out_specs=pl.BlockSpec((tm, tn), lambda j, i: (i, j)),
        out_shape=jax.ShapeDtypeStruct((m, n_out), out_dtype),
        scratch_shapes=[pltpu.VMEM((k, tn), BF16)],
        compiler_params=_cparams("arbitrary", "arbitrary"),
        name="matmul_in",
    )(*args)


def _outproj_kernel(*refs, has_bias, modulate, write_x):
    a_ref, w_ref, x_ref, gate_ref = refs[:4]
    pos = 4
    y = jnp.dot(a_ref[...], w_ref[...], preferred_element_type=F32)
    if has_bias:
        y = y + refs[pos][...]
        pos += 1
    g_ref, sc_ref, sh_ref = refs[pos:pos + 3]
    xn = x_ref[...] + gate_ref[...] * y
    if write_x:
        refs[-2][...] = xn
    hn = xn * lax.rsqrt(jnp.mean(xn * xn, axis=-1, keepdims=True) + EPS) * g_ref[...]
    if modulate:
        hn = hn * (1.0 + sc_ref[...]) + sh_ref[...]
    refs[-1][...] = hn.astype(refs[-1].dtype)


def _outproj(a, w, x, gate, bias, gain, scale, shift, *, modulate, write_x, out_dtype, tm=256):
    m, k = a.shape
    d = w.shape[1]
    row = pl.BlockSpec((1, d), lambda i: (0, 0))
    tile = pl.BlockSpec((tm, d), lambda i: (i, 0))
    in_specs = [pl.BlockSpec((tm, k), lambda i: (i, 0)), pl.BlockSpec((k, d), lambda i: (0, 0)), tile, row]
    args = [a, w, x, gate]
    if bias is not None:
        in_specs.append(row)
        args.append(bias)
    in_specs += [row, row, row]
    args += [gain, scale, shift]
    out_specs = [tile]
    out_shape = [jax.ShapeDtypeStruct((m, d), out_dtype)]
    if write_x:
        out_specs = [tile, tile]
        out_shape = [jax.ShapeDtypeStruct((m, d), F32)] + out_shape
    return pl.pallas_call(
        functools.partial(_outproj_kernel, has_bias=bias is not None, modulate=modulate, write_x=write_x),
        grid=(m // tm,),
        in_specs=in_specs,
        out_specs=out_specs,
        out_shape=out_shape,
        compiler_params=_cparams("arbitrary"),
        name="matmul_out",
    )(*args)


def _attn_kernel(slope_ref, lam_ref, q_ref, k_ref, v_ref, g_ref, sg_ref, o_ref, acc_ref, l_ref, s_ref,
                 *, tq, out_scale):
    h = pl.program_id(0)
    i = pl.program_id(1)
    dk = DA_QK_DIM
    slope = slope_ref[h]
    lam = lam_ref[0]
    acc_ref[...] = jnp.zeros_like(acc_ref)
    l_ref[...] = jnp.zeros_like(l_ref)
    lanes = 128
    q0 = i * tq

    def scores(mi, k0, tk, m_old, masked):
        colf = lax.broadcasted_iota(jnp.int32, (1, tk), 1).astype(F32)
        bias = slope * (colf + (k0 - q0).astype(F32))
        qm = q_ref[:, mi * dk:(mi + 1) * dk]
        km = k_ref[pl.ds(k0, tk), mi * dk:(mi + 1) * dk]
        s = lax.dot_general(qm, km, (((1,), (1,)), ((), ())), preferred_element_type=F32) + bias
        if masked:
            r = lax.broadcasted_iota(jnp.int32, (tq, tk), 0)
            c = lax.broadcasted_iota(jnp.int32, (tq, tk), 1)
            s = jnp.where(r >= c, s, -jnp.inf)
        s_ref[mi, :, 0:tk] = s
        return jnp.maximum(m_old, jnp.max(s, axis=-1, keepdims=True))

    def accumulate(mi, k0, tk, m_old, m_new):
        vb = v_ref[pl.ds(k0, tk), :]
        alpha = jnp.exp2(m_old - m_new)
        p = jnp.exp2(s_ref[mi, :, 0:tk] - m_new)
        psum = p[:, 0:lanes]
        for t in range(1, tk // lanes):
            psum = psum + p[:, t * lanes:(t + 1) * lanes]
        l_ref[mi] = alpha * l_ref[mi] + psum
        acc_ref[mi] = alpha * acc_ref[mi] + jnp.dot(p.astype(BF16), vb, preferred_element_type=F32)

    def tile(k0, tk, m, masked):
        k0 = pl.multiple_of(k0, tq)
        m_new = tuple(scores(mi, k0, tk, m[mi], masked) for mi in range(2))
        for mi in range(2):
            accumulate(mi, k0, tk, m[mi], m_new[mi])
        return m_new

    neg = jnp.full((tq, 1), -1e30, F32)
    m = tile(q0, tq, (neg, neg), True)
    wide = s_ref.shape[2]
    w = tq
    end = q0
    while w < wide:
        take = (i // (w // tq)) % 2 == 1
        m = lax.cond(take, functools.partial(lambda m, w, end: tile(end - w, w, m, False), w=w, end=end),
                     lambda m: m, m)
        end = end - jnp.where(take, w, 0)
        w *= 2
    lax.fori_loop(0, i // (wide // tq), lambda jj, m: tile(jj * wide, wide, m, False), m)

    l1 = jnp.sum(l_ref[0], axis=-1, keepdims=True)
    l2 = jnp.sum(l_ref[1], axis=-1, keepdims=True)
    o = acc_ref[0] / l1 - lam * (acc_ref[1] / l2)
    y = o * lax.rsqrt(jnp.mean(o * o, axis=-1, keepdims=True) + EPS) * sg_ref[...] * out_scale
    o_ref[...] = (y * _silu(g_ref[...].astype(F32))).astype(o_ref.dtype)


def _attention(u, slopes, lam, subln_g, *, out_scale, tq=512, tk_wide=2048):
    s = u.shape[0]
    dv = DA_V_DIM
    nh = DA_HEADS
    smem = pl.BlockSpec(memory_space=pltpu.SMEM)
    return pl.pallas_call(
        functools.partial(_attn_kernel, tq=tq, out_scale=out_scale),
        grid=(nh, s // tq),
        in_specs=[
            smem, smem,
            pl.BlockSpec((tq, dv), lambda h, i: (i, h)),
            pl.BlockSpec((s, dv), lambda h, i: (0, nh + h)),
            pl.BlockSpec((s, dv), lambda h, i: (0, 2 * nh + h)),
            pl.BlockSpec((tq, dv), lambda h, i: (i, 3 * nh + h)),
            pl.BlockSpec((1, dv), lambda h, i: (0, 0)),
        ],
        out_specs=pl.BlockSpec((tq, dv), lambda h, i: (i, h)),
        out_shape=jax.ShapeDtypeStruct((s, nh * dv), BF16),
        scratch_shapes=[pltpu.VMEM((2, tq, dv), F32), pltpu.VMEM((2, tq, 128), F32),
                        pltpu.VMEM((2, tq, tk_wide), F32)],
        compiler_params=_cparams("arbitrary", "arbitrary"),
        name="diff_attention",
    )(slopes, lam, u, u, u, u, subln_g)


def _hgrn_exponent_matrix():
    c = HG_CHUNK
    w = np.zeros((len(HG_LEVELS) + 1, c, c), np.float32)
    for r in range(c):
        w[0, r, :r + 1] = 1.0
        for li, m in enumerate(HG_LEVELS):
            mid = (r // (2 * m)) * 2 * m + m
            if r % (2 * m) >= m:
                w[li + 1, r, mid:r + 1] = 1.0
            else:
                w[li + 1, r, r + 1:mid] = 1.0
    return w.reshape(-1, c)


def _hgrn_kernel(q_ref, f_ref, i_ref, lb_ref, w_ref, o_ref, st_ref, st0_ref, *, tt):
    c = HG_CHUNK

    @pl.when(pl.program_id(1) == 0)
    def _():
        st_ref[...] = jnp.zeros_like(st_ref)

    lb = lb_ref[...]
    wall = w_ref[...]
    row = lax.broadcasted_iota(jnp.int32, (c, 1), 0)
    r2 = lax.broadcasted_iota(jnp.int32, (c, c), 0)
    c2 = lax.broadcasted_iota(jnp.int32, (c, c), 1)
    row8 = lax.broadcasted_iota(jnp.int32, (8, 1), 0)
    nt = (((1,), (1,)), ((), ()))
    tn = (((0,), (0,)), ((), ()))

    def chunk(ci, carry):
        r0 = pl.multiple_of(ci * c, c)
        q = _silu(q_ref[pl.ds(r0, c), :])
        v = i_ref[pl.ds(r0, c), :]
        fg = lb + (1.0 - lb) * jax.nn.sigmoid(f_ref[pl.ds(r0, c), :])
        kk = 1.0 - fg
        g = jnp.log(fg)
        g_hi = g.astype(BF16)
        g_lo = (g - g_hi.astype(F32)).astype(BF16)
        ex = (jnp.dot(wall, g_hi, preferred_element_type=F32)
              + jnp.dot(wall, g_lo, preferred_element_type=F32))
        b = ex[0:c]
        st = st_ref[...]
        vb = v.astype(BF16)

        o = lax.dot_general((q * jnp.exp(b)).astype(BF16), st.astype(BF16), nt,
                            preferred_element_type=F32)

        a = jnp.zeros((c, c), F32)
        for li, m in enumerate(HG_LEVELS):
            e = jnp.exp(ex[(li + 1) * c:(li + 2) * c])
            upper = (row % (2 * m)) >= m
            qt = jnp.where(upper, q * e, 0.0).astype(BF16)
            kt = jnp.where(upper, 0.0, kk * e).astype(BF16)
            al = lax.dot_general(qt, kt, nt, preferred_element_type=F32)
            if 2 * m != c:
                al = jnp.where((r2 // (2 * m)) == (c2 // (2 * m)), al, 0.0)
            a = a + al
        o = o + jnp.dot(a.astype(BF16), vb, preferred_element_type=F32)

        parts = []
        for blk in range(c // 8):
            sl = slice(8 * blk, 8 * blk + 8)
            bb, kb, v8, q8 = b[sl], kk[sl], v[sl], q[sl]
            od = jnp.zeros((8, HG_DIM), F32)
            for s in range(8):
                e = jnp.exp(jnp.where(row8 >= s, bb - bb[s:s + 1], -jnp.inf))
                w = jnp.sum(e * q8 * kb[s:s + 1], axis=-1, keepdims=True)
                od = od + w * v8[s:s + 1]
            parts.append(od)
        o_ref[pl.ds(r0, c), :] = o + jnp.concatenate(parts, axis=0)

        b_last = b[c - 1:c]
        kd = (kk * jnp.exp(b_last - b)).astype(BF16)
        st_ref[...] = jnp.exp(b_last) * st + lax.dot_general(vb, kd, tn, preferred_element_type=F32)
        return carry

    n_chunks = tt // c
    dk = HG_DIM
    st0_ref[...] = st_ref[...]

    def side_by_side(fn):
        return jnp.concatenate([fn(n * c) for n in range(n_chunks)], axis=1)

    fg = side_by_side(lambda r0: lb + (1.0 - lb) * jax.nn.sigmoid(f_ref[r0:r0 + c, :]))
    g = jnp.log(fg)
    g_hi = g.astype(BF16)
    g_lo = (g - g_hi.astype(F32)).astype(BF16)
    tril = wall[0:c]
    b = (jnp.dot(tril, g_hi, preferred_element_type=F32)
         + jnp.dot(tril, g_lo, preferred_element_type=F32))
    safe = jnp.min(b[c - 1:c, :]) >= -HG_SAFE_EXP
    eb = jnp.exp(b)
    eb_last = eb[c - 1:c, :]
    qb = (side_by_side(lambda r0: _silu(q_ref[r0:r0 + c, :])) * eb).astype(BF16)
    ke = (1.0 - fg) * jnp.exp(-b)
    keb = ke.astype(BF16)
    kd = (ke * eb_last).astype(BF16)
    vbs = [i_ref[n * c:(n + 1) * c, :].astype(BF16) for n in range(n_chunks)]
    lanes_of = lambda x, n: x[:, n * dk:(n + 1) * dk]
    causal = r2 >= c2
    a = [jnp.where(causal, lax.dot_general(lanes_of(qb, n), lanes_of(keb, n), nt, preferred_element_type=F32),
                   0.0).astype(BF16) for n in range(n_chunks)]
    o_intra = [jnp.dot(a[n], vbs[n], preferred_element_type=F32) for n in range(n_chunks)]
    upd = [lax.dot_general(vbs[n], lanes_of(kd, n), tn, preferred_element_type=F32) for n in range(n_chunks)]
    st = st_ref[...]
    for n in range(n_chunks):
        o_ref[n * c:(n + 1) * c, :] = o_intra[n] + lax.dot_general(
            lanes_of(qb, n), st.astype(BF16), nt, preferred_element_type=F32)
        st = lanes_of(eb_last, n) * st + upd[n]
    st_ref[...] = st

    @pl.when(jnp.logical_not(safe))
    def _():
        st_ref[...] = st0_ref[...]
        lax.fori_loop(0, n_chunks, chunk, 0, unroll=2)


def _hgrn_core(u, lb, *, tt=1024):
    s = u.shape[0]
    nh, dk = HG_HEADS, HG_DIM
    wall = jnp.asarray(_hgrn_exponent_matrix(), BF16)
    return pl.pallas_call(
        functools.partial(_hgrn_kernel, tt=tt),
        grid=(nh, s // tt),
        in_specs=[
            pl.BlockSpec((tt, dk), lambda h, t: (t, h)),
            pl.BlockSpec((tt, dk), lambda h, t: (t, nh + h)),
            pl.BlockSpec((tt, dk), lambda h, t: (t, 2 * nh + h)),
            pl.BlockSpec((1, dk), lambda h, t: (0, h)),
            pl.BlockSpec(wall.shape, lambda h, t: (0, 0)),
        ],
        out_specs=pl.BlockSpec((tt, dk), lambda h, t: (t, h)),
        out_shape=jax.ShapeDtypeStruct((s, nh * dk), F32),
        scratch_shapes=[pltpu.VMEM((dk, dk), F32), pltpu.VMEM((dk, dk), F32)],
        compiler_params=_cparams("arbitrary", "arbitrary"),
        name="hgrn2_core",
    )(u, u, u, lb, wall)


def _gnorm_kernel(o_ref, gate_ref, g_ref, out_ref):
    o = o_ref[...]
    y = o * lax.rsqrt(jnp.mean(o * o, axis=-1, keepdims=True) + EPS) * g_ref[...]
    out_ref[...] = (y * _silu(gate_ref[...])).astype(out_ref.dtype)


def _gnorm_gate(o, u, gain, *, tm=512):
    s, d = o.shape
    gate_blk = (u.shape[1] - d) // d
    return pl.pallas_call(
        _gnorm_kernel,
        grid=(s // tm,),
        in_specs=[
            pl.BlockSpec((tm, d), lambda i: (i, 0)),
            pl.BlockSpec((tm, d), lambda i: (i, gate_blk)),
            pl.BlockSpec((1, d), lambda i: (0, 0)),
        ],
        out_specs=pl.BlockSpec((tm, d), lambda i: (i, 0)),
        out_shape=jax.ShapeDtypeStruct((s, d), BF16),
        compiler_params=_cparams("arbitrary"),
        name="hgrn2_gnorm_gate",
    )(o, u, gain)


def _conv_kernel(a_ref, ag_ref, gate_ref, dw_ref, dwb_ref, lng_ref, lnb_ref, o_ref, ybuf, cbuf, shbuf,
                 *, tt, cw):
    halo = CONV_HALO
    d = a_ref.shape[1]

    @pl.when(pl.program_id(0) == 0)
    def _():
        ybuf[0:halo, :] = jnp.zeros((halo, d), F32)

    @pl.when(pl.program_id(0) > 0)
    def _():
        ybuf[0:halo, :] = ybuf[tt:tt + halo, :]

    strip = 16

    def glu(si, carry):
        r = pl.multiple_of(si * strip, strip)
        ybuf[pl.ds(pl.multiple_of(halo + r, strip), strip), :] = (
            a_ref[pl.ds(r, strip), :] * _sigmoid(ag_ref[pl.ds(r, strip), :]))
        return carry

    lax.fori_loop(0, tt // strip, glu, 0)
    first = halo - (CONV_WIDTH - 1)

    def lanes(ci, carry):
        c0 = pl.multiple_of(ci * cw, cw)
        acc = jnp.broadcast_to(dwb_ref[:, pl.ds(c0, cw)], (tt, cw))
        for rho in range(8):
            taps = [k for k in range(CONV_WIDTH) if (first + k) % 8 == rho]
            if not taps:
                continue
            span = tt + 8 * max((first + k) // 8 for k in taps)
            if rho:
                shbuf[rho, 0:span, :] = ybuf[rho:rho + span, pl.ds(c0, cw)]
            for k in taps:
                a8 = 8 * ((first + k) // 8)
                yk = shbuf[rho, a8:a8 + tt, :] if rho else ybuf[a8:a8 + tt, pl.ds(c0, cw)]
                acc = acc + yk * dw_ref[k:k + 1, pl.ds(c0, cw)]
        cbuf[:, pl.ds(c0, cw)] = acc
        return carry

    lax.fori_loop(0, d // cw, lanes, 0)

    def layernorm_gate(si, carry):
        r = pl.multiple_of(si * strip, strip)
        y = cbuf[pl.ds(r, strip), :]
        mu = jnp.mean(y, axis=-1, keepdims=True)
        yc = y - mu
        var = jnp.mean(yc * yc, axis=-1, keepdims=True)
        z = yc * lax.rsqrt(var + EPS) * lng_ref[...] + lnb_ref[...]
        o_ref[pl.ds(r, strip), :] = (_silu(z) * _silu(gate_ref[pl.ds(r, strip), :])).astype(o_ref.dtype)
        return carry

    lax.fori_loop(0, tt // strip, layernorm_gate, 0, unroll=2)


def _conv_core(u, dw, dw_b, ln_g, ln_b, *, tt=256, cw=128):
    s = u.shape[0]
    d = dw.shape[1]
    row = pl.BlockSpec((1, d), lambda t: (0, 0))
    return pl.pallas_call(
        functools.partial(_conv_kernel, tt=tt, cw=cw),
        grid=(s // tt,),
        in_specs=[
            pl.BlockSpec((tt, d), lambda t: (t, 0)),
            pl.BlockSpec((tt, d), lambda t: (t, 1)),
            pl.BlockSpec((tt, d), lambda t: (t, 2)),
            pl.BlockSpec(dw.shape, lambda t: (0, 0)),
            row, row, row,
        ],
        out_specs=pl.BlockSpec((tt, d), lambda t: (t, 0)),
        out_shape=jax.ShapeDtypeStruct((s, d), BF16),
        scratch_shapes=[pltpu.VMEM((tt + CONV_HALO, d), F32), pltpu.VMEM((tt, d), F32),
                        pltpu.VMEM((8, tt + CONV_HALO, cw), F32)],
        compiler_params=_cparams("arbitrary"),
        name="conv_module_core",
    )(u, u, u, dw, dw_b, ln_g, ln_b)


def kernel(x, c, norm_g, ada_w, ada_b, attn_w_in, attn_w_out, attn_lam_q1, attn_lam_k1, attn_lam_q2, attn_lam_k2, attn_subln_g, hgrn_w_in, hgrn_w_out, hgrn_lb_logits, hgrn_gnorm_g, conv_w_in, conv_b_in, conv_dw, conv_dw_b, conv_ln_g, conv_ln_b, conv_w_out, conv_b_out, final_g):
    bsz, s, d = x.shape
    depth = norm_g.shape[0]
    assert bsz == 1 and c.shape == (1, d)
    xs = x.reshape(s, d)
    mods = _mods(c, ada_w, ada_b)

    slopes = (2.0 ** (-8.0 * jnp.arange(1, DA_HEADS + 1, dtype=F32) / DA_HEADS)) * LOG2E
    qk_cols = 2 * DA_HEADS * DA_QK_DIM
    attn_colscale = jnp.concatenate(
        [jnp.full((1, qk_cols), DA_QK_DIM ** -0.5 * LOG2E, F32), jnp.ones((1, 3 * qk_cols), F32)], axis=1)
    lb_all = jax.nn.softmax(hgrn_lb_logits.astype(F32), axis=0)
    lb_all = jnp.cumsum(lb_all, axis=0) - lb_all[0]

    def mod_rows(i):
        return tuple(mods[i, :, k * d:(k + 1) * d] for k in range(3))

    shift, scale, gate = mod_rows(0)
    h = _norm(xs, norm_g[0:1], scale, shift, modulate=True, out_dtype=BF16)
    for i in range(depth):
        kind, j = i % N_MIXERS, i // N_MIXERS
        bias_out = None
        if kind == 0:
            u = _matmul(h, attn_w_in, j, out_dtype=BF16, colscale=attn_colscale)
            lam_init = 0.8 - 0.6 * math.exp(-0.3 * i)
            lam = (jnp.exp(jnp.sum(attn_lam_q1[j] * attn_lam_k1[j]))
                   - jnp.exp(jnp.sum(attn_lam_q2[j] * attn_lam_k2[j])) + lam_init).reshape(1)
            y = _attention(u, slopes, lam, attn_subln_g[j:j + 1], out_scale=1.0 - lam_init)
            w_out = attn_w_out[j]
        elif kind == 1:
            u = _matmul(h, hgrn_w_in, j, out_dtype=F32)
            o = _hgrn_core(u, lb_all[i:i + 1])
            y = _gnorm_gate(o, u, hgrn_gnorm_g[j:j + 1])
            w_out = hgrn_w_out[j]
        else:
            u = _matmul(h, conv_w_in, j, out_dtype=F32, bias=conv_b_in[j:j + 1])
            y = _conv_core(u, conv_dw[j], conv_dw_b[j:j + 1], conv_ln_g[j:j + 1], conv_ln_b[j:j + 1])
            w_out = conv_w_out[j]
            bias_out = conv_b_out[j:j + 1]
        if i + 1 < depth:
            shift_n, scale_n, gate_n = mod_rows(i + 1)
            xs, h = _outproj(y, w_out.astype(BF16), xs, gate, bias_out, norm_g[i + 1:i + 2], scale_n, shift_n,
                             modulate=True, write_x=True, out_dtype=BF16)
            gate = gate_n
        else:
            zero = jnp.zeros((1, d), F32)
            out, = _outproj(y, w_out.astype(BF16), xs, gate, bias_out, final_g.reshape(1, d), zero, zero,
                            modulate=False, write_x=False, out_dtype=F32)
    return out.reshape(bsz, s, d)
```

```python
import functools
import math

import numpy as np
import jax
import jax.numpy as jnp
from jax import lax
from jax.experimental import pallas as pl
from jax.experimental.pallas import tpu as pltpu

F32 = jnp.float32
BF16 = jnp.bfloat16

EPS = 1e-6
LOG2E = 1.4426950408889634
N_MIXERS = 3

DA_HEADS = 8
DA_QK_DIM = 128
DA_V_DIM = 256
HG_HEADS = 16
HG_DIM = 128
HG_CHUNK = 64
HG_LEVELS = (32, 16, 8)
HG_SAFE_EXP = 80.0
CONV_WIDTH = 31
CONV_HALO = 32

VMEM_LIMIT_BYTES = 56 * 1024 * 1024


def _cparams(*sem):
    return pltpu.CompilerParams(dimension_semantics=sem, vmem_limit_bytes=VMEM_LIMIT_BYTES)


def _sigmoid(x):
    return 0.5 * jnp.tanh(0.5 * x) + 0.5


def _silu(x):
    return x * _sigmoid(x)


def _mods_kernel(c_ref, w_ref, b_ref, o_ref, ca_ref, *, kc):
    c = c_ref[...]
    ca_ref[...] = _silu(c)
    d = c.shape[0]
    tn = w_ref.shape[2]

    def body(i, acc):
        k0 = pl.multiple_of(i * kc, kc)
        prod = w_ref[0, pl.ds(k0, kc), :] * ca_ref[pl.ds(k0, kc), :]
        return acc + prod.reshape(kc // 8, 8, tn).sum(axis=0)

    acc = lax.fori_loop(0, d // kc, body, jnp.zeros((8, tn), F32))
    o_ref[0] = jnp.sum(acc, axis=0, keepdims=True) + b_ref[0]


def _mods(c, ada_w, ada_b, *, tn=1024, kc=64):
    depth, d, n = ada_w.shape
    return pl.pallas_call(
        functools.partial(_mods_kernel, kc=kc),
        grid=(depth, n // tn),
        in_specs=[
            pl.BlockSpec((d, 1), lambda i, j: (0, 0)),
            pl.BlockSpec((1, d, tn), lambda i, j: (i, 0, j)),
            pl.BlockSpec((1, 1, tn), lambda i, j: (i, 0, j)),
        ],
        out_specs=pl.BlockSpec((1, 1, tn), lambda i, j: (i, 0, j)),
        out_shape=jax.ShapeDtypeStruct((depth, 1, n), F32),
        scratch_shapes=[pltpu.VMEM((d, 1), F32)],
        compiler_params=_cparams("arbitrary", "arbitrary"),
        name="adaln_mods",
    )(c.reshape(d, 1), ada_w, ada_b.reshape(depth, 1, n))


def _norm_kernel(x_ref, g_ref, sc_ref, sh_ref, o_ref, *, modulate):
    x = x_ref[...]
    y = x * lax.rsqrt(jnp.mean(x * x, axis=-1, keepdims=True) + EPS) * g_ref[...]
    if modulate:
        y = y * (1.0 + sc_ref[...]) + sh_ref[...]
    o_ref[...] = y.astype(o_ref.dtype)


def _norm(x, gain, scale, shift, *, modulate, out_dtype, tm=512):
    s, d = x.shape
    row = pl.BlockSpec((1, d), lambda i: (0, 0))
    return pl.pallas_call(
        functools.partial(_norm_kernel, modulate=modulate),
        grid=(s // tm,),
        in_specs=[pl.BlockSpec((tm, d), lambda i: (i, 0)), row, row, row],
        out_specs=pl.BlockSpec((tm, d), lambda i: (i, 0)),
        out_shape=jax.ShapeDtypeStruct((s, d), out_dtype),
        compiler_params=_cparams("arbitrary"),
        name="rmsnorm_mod",
    )(x, gain, scale, shift)


def _mm_kernel(*refs, has_bias, has_colscale):
    a_ref, w_ref = refs[0], refs[1]
    o_ref, wb_ref = refs[-2], refs[-1]
    pos = 2

    @pl.when(pl.program_id(1) == 0)
    def _():
        wb_ref[...] = w_ref[...].astype(BF16)

    acc = jnp.dot(a_ref[...], wb_ref[...], preferred_element_type=F32)
    if has_bias:
        acc = acc + refs[pos][...]
        pos += 1
    if has_colscale:
        acc = acc * refs[pos][...]
    o_ref[...] = acc.astype(o_ref.dtype)


def _matmul(a, w, layer, *, out_dtype, bias=None, colscale=None, tm=1024, tn=1024):
    m, k = a.shape
    n_out = w.shape[2]
    in_specs = [
        pl.BlockSpec((tm, k), lambda j, i: (i, 0)),
        pl.BlockSpec((None, k, tn), lambda j, i: (layer, 0, j)),
    ]
    args = [a, w]
    rowspec = pl.BlockSpec((1, tn), lambda j, i: (0, j))
    for extra in (bias, colscale):
        if extra is not None:
            in_specs.append(rowspec)
            args.append(extra)
    return pl.pallas_call(
        functools.partial(_mm_kernel, has_bias=bias is not None, has_colscale=colscale is not None),
        grid=(n_out // tn, m // tm),
        in_specs=in_specs,
        out_specs=pl.BlockSpec((tm, tn), lambda j, i: (i, j)),
        out_shape=jax.ShapeDtypeStruct((m, n_out), out_dtype),
        scratch_shapes=[pltpu.VMEM((k, tn), BF16)],
        compiler_params=_cparams("arbitrary", "arbitrary"),
        name="matmul_in",
    )(*args)


def _outproj_kernel(*refs, has_bias, has_pre, modulate, write_x):
    a_ref, w_ref, x_ref, gate_ref = refs[:4]
    pos = 4
    a = a_ref[...]
    if has_pre:
        a = a * lax.rsqrt(jnp.mean(a * a, axis=-1, keepdims=True) + EPS) * refs[pos + 1][...]
        a = (a * _silu(refs[pos][...])).astype(BF16)
        pos += 2
    y = jnp.dot(a, w_ref[...], preferred_element_type=F32)
    if has_bias:
        y = y + refs[pos][...]
        pos += 1
    g_ref, sc_ref, sh_ref = refs[pos:pos + 3]
    xn = x_ref[...] + gate_ref[...] * y
    if write_x:
        refs[-2][...] = xn
    hn = xn * lax.rsqrt(jnp.mean(xn * xn, axis=-1, keepdims=True) + EPS) * g_ref[...]
    if modulate:
        hn = hn * (1.0 + sc_ref[...]) + sh_ref[...]
    refs[-1][...] = hn.astype(refs[-1].dtype)


def _outproj(a, w, x, gate, bias, gain, scale, shift, *, modulate, write_x, out_dtype, pre=None, tm=None):
    m, k = a.shape
    d = w.shape[1]
    if tm is None:
        tm = 256 if pre is not None else 512
    row = pl.BlockSpec((1, d), lambda i: (0, 0))
    tile = pl.BlockSpec((tm, d), lambda i: (i, 0))
    in_specs = [pl.BlockSpec((tm, k), lambda i: (i, 0)), pl.BlockSpec((k, d), lambda i: (0, 0)), tile, row]
    args = [a, w, x, gate]
    if pre is not None:
        u, col_block, pre_gain = pre
        in_specs += [pl.BlockSpec((tm, k), lambda i: (i, col_block)), pl.BlockSpec((1, k), lambda i: (0, 0))]
        args += [u, pre_gain]
    if bias is not None:
        in_specs.append(row)
        args.append(bias)
    in_specs += [row, row, row]
    args += [gain, scale, shift]
    out_specs = [tile]
    out_shape = [jax.ShapeDtypeStruct((m, d), out_dtype)]
    if write_x:
        out_specs = [tile, tile]
        out_shape = [jax.ShapeDtypeStruct((m, d), F32)] + out_shape
    return pl.pallas_call(
        functools.partial(_outproj_kernel, has_bias=bias is not None, has_pre=pre is not None,
                          modulate=modulate, write_x=write_x),
        grid=(m // tm,),
        in_specs=in_specs,
        out_specs=out_specs,
        out_shape=out_shape,
        compiler_params=_cparams("arbitrary"),
        name="matmul_out",
    )(*args)


def _attn_kernel(slope_ref, lam_ref, q_ref, k_ref, v_ref, g_ref, sg_ref, o_ref, acc_ref, l_ref, s_ref,
                 *, tq, out_scale):
    h = pl.program_id(0)
    i = pl.program_id(1)
    dk = DA_QK_DIM
    slope = slope_ref[h]
    lam = lam_ref[0]
    lanes = 128
    q0 = i * tq

    def scores(mi, k0, tk, m_old, diag_col):
        colf = lax.broadcasted_iota(jnp.int32, (1, tk), 1).astype(F32)
        bias = slope * (colf + (k0 - q0).astype(F32))
        qm = q_ref[:, mi * dk:(mi + 1) * dk]
        km = k_ref[pl.ds(k0, tk), mi * dk:(mi + 1) * dk]
        s = lax.dot_general(qm, km, (((1,), (1,)), ((), ())), preferred_element_type=F32) + bias
        if diag_col is not None:
            r = lax.broadcasted_iota(jnp.int32, (tq, tk), 0)
            c = lax.broadcasted_iota(jnp.int32, (tq, tk), 1)
            s = jnp.where(r + diag_col >= c, s, -jnp.inf)
        s_ref[mi, :, 0:tk] = s
        smax = jnp.max(s, axis=-1, keepdims=True)
        return smax if m_old is None else jnp.maximum(m_old, smax)

    def accumulate(mi, k0, tk, m_old, m_new):
        vb = v_ref[pl.ds(k0, tk), :]
        p = jnp.exp2(s_ref[mi, :, 0:tk] - m_new)
        psum = p[:, 0:lanes]
        for t in range(1, tk // lanes):
            psum = psum + p[:, t * lanes:(t + 1) * lanes]
        if m_old is None:
            l_ref[mi] = psum
            acc_ref[mi] = jnp.dot(p.astype(BF16), vb, preferred_element_type=F32)
        else:
            alpha = jnp.exp2(m_old - m_new)
            l_ref[mi] = alpha * l_ref[mi] + psum
            acc_ref[mi] = alpha * acc_ref[mi] + jnp.dot(p.astype(BF16), vb, preferred_element_type=F32)

    def tile(k0, tk, m, diag_col=None):
        k0 = pl.multiple_of(k0, tq)
        old = (None, None) if m is None else m
        new = tuple(scores(mi, k0, tk, old[mi], diag_col) for mi in range(2))
        for mi in range(2):
            accumulate(mi, k0, tk, old[mi], new[mi])
        return new

    odd = i % 2 == 1
    m = lax.cond(odd, lambda: tile(q0 - tq, 2 * tq, None, diag_col=tq), lambda: tile(q0, tq, None, diag_col=0))
    wide = s_ref.shape[2]
    w = 2 * tq
    end = q0 - jnp.where(odd, tq, 0)
    while w < wide:
        take = (i // (w // tq)) % 2 == 1
        m = lax.cond(take, functools.partial(lambda m, w, end: tile(end - w, w, m), w=w, end=end),
                     lambda m: m, m)
        end = end - jnp.where(take, w, 0)
        w *= 2
    lax.fori_loop(0, i // (wide // tq), lambda jj, m: tile(jj * wide, wide, m), m)

    inv1 = 1.0 / jnp.sum(l_ref[0], axis=-1, keepdims=True)
    inv2 = lam / jnp.sum(l_ref[1], axis=-1, keepdims=True)
    o = acc_ref[0] * inv1 - acc_ref[1] * inv2
    y = o * lax.rsqrt(jnp.mean(o * o, axis=-1, keepdims=True) + EPS) * sg_ref[...] * out_scale
    o_ref[...] = (y * _silu(g_ref[...].astype(F32))).astype(o_ref.dtype)


def _attention(u, slopes, lam, subln_g, *, out_scale, tq=512, tk_wide=2048):
    s = u.shape[0]
    dv = DA_V_DIM
    nh = DA_HEADS
    smem = pl.BlockSpec(memory_space=pltpu.SMEM)
    return pl.pallas_call(
        functools.partial(_attn_kernel, tq=tq, out_scale=out_scale),
        grid=(nh, s // tq),
        in_specs=[
            smem, smem,
            pl.BlockSpec((tq, dv), lambda h, i: (i, h)),
            pl.BlockSpec((s, dv), lambda h, i: (0, nh + h)),
            pl.BlockSpec((s, dv), lambda h, i: (0, 2 * nh + h)),
            pl.BlockSpec((tq, dv), lambda h, i: (i, 3 * nh + h)),
            pl.BlockSpec((1, dv), lambda h, i: (0, 0)),
        ],
        out_specs=pl.BlockSpec((tq, dv), lambda h, i: (i, h)),
        out_shape=jax.ShapeDtypeStruct((s, nh * dv), BF16),
        scratch_shapes=[pltpu.VMEM((2, tq, dv), F32), pltpu.VMEM((2, tq, 128), F32),
                        pltpu.VMEM((2, tq, tk_wide), F32)],
        compiler_params=_cparams("arbitrary", "arbitrary"),
        name="diff_attention",
    )(slopes, lam, u, u, u, u, subln_g)


def _hgrn_exponent_matrix():
    c = HG_CHUNK
    w = np.zeros((len(HG_LEVELS) + 1, c, c), np.float32)
    for r in range(c):
        w[0, r, :r + 1] = 1.0
        for li, m in enumerate(HG_LEVELS):
            mid = (r // (2 * m)) * 2 * m + m
            if r % (2 * m) >= m:
                w[li + 1, r, mid:r + 1] = 1.0
            else:
                w[li + 1, r, r + 1:mid] = 1.0
    return w.reshape(-1, c)


def _hgrn_kernel(q_ref, f_ref, i_ref, lb_ref, w_ref, o_ref, st_ref, st0_ref, *, tt):
    c = HG_CHUNK

    @pl.when(pl.program_id(1) == 0)
    def _():
        st_ref[...] = jnp.zeros_like(st_ref)

    lb = lb_ref[...]
    wall = w_ref[...]
    row = lax.broadcasted_iota(jnp.int32, (c, 1), 0)
    r2 = lax.broadcasted_iota(jnp.int32, (c, c), 0)
    c2 = lax.broadcasted_iota(jnp.int32, (c, c), 1)
    row8 = lax.broadcasted_iota(jnp.int32, (8, 1), 0)
    nt = (((1,), (1,)), ((), ()))
    tn = (((0,), (0,)), ((), ()))

    def chunk(ci, carry):
        r0 = pl.multiple_of(ci * c, c)
        q = _silu(q_ref[pl.ds(r0, c), :])
        v = i_ref[pl.ds(r0, c), :]
        fg = lb + (1.0 - lb) * jax.nn.sigmoid(f_ref[pl.ds(r0, c), :])
        kk = 1.0 - fg
        g = jnp.log(fg)
        g_hi = g.astype(BF16)
        g_lo = (g - g_hi.astype(F32)).astype(BF16)
        ex = (jnp.dot(wall, g_hi, preferred_element_type=F32)
              + jnp.dot(wall, g_lo, preferred_element_type=F32))
        b = ex[0:c]
        st = st_ref[...]
        vb = v.astype(BF16)

        o = lax.dot_general((q * jnp.exp(b)).astype(BF16), st.astype(BF16), nt,
                            preferred_element_type=F32)

        a = jnp.zeros((c, c), F32)
        for li, m in enumerate(HG_LEVELS):
            e = jnp.exp(ex[(li + 1) * c:(li + 2) * c])
            upper = (row % (2 * m)) >= m
            qt = jnp.where(upper, q * e, 0.0).astype(BF16)
            kt = jnp.where(upper, 0.0, kk * e).astype(BF16)
            al = lax.dot_general(qt, kt, nt, preferred_element_type=F32)
            if 2 * m != c:
                al = jnp.where((r2 // (2 * m)) == (c2 // (2 * m)), al, 0.0)
            a = a + al
        o = o + jnp.dot(a.astype(BF16), vb, preferred_element_type=F32)

        parts = []
        for blk in range(c // 8):
            sl = slice(8 * blk, 8 * blk + 8)
            bb, kb, v8, q8 = b[sl], kk[sl], v[sl], q[sl]
            od = jnp.zeros((8, HG_DIM), F32)
            for s in range(8):
                e = jnp.exp(jnp.where(row8 >= s, bb - bb[s:s + 1], -jnp.inf))
                w = jnp.sum(e * q8 * kb[s:s + 1], axis=-1, keepdims=True)
                od = od + w * v8[s:s + 1]
            parts.append(od)
        o_ref[pl.ds(r0, c), :] = o + jnp.concatenate(parts, axis=0)

        b_last = b[c - 1:c]
        kd = (kk * jnp.exp(b_last - b)).astype(BF16)
        st_ref[...] = jnp.exp(b_last) * st + lax.dot_general(vb, kd, tn, preferred_element_type=F32)
        return carry

    n_chunks = tt // c
    dk = HG_DIM
    st0_ref[...] = st_ref[...]

    def side_by_side(fn):
        return jnp.concatenate([fn(n * c) for n in range(n_chunks)], axis=1)

    fg = side_by_side(lambda r0: lb + (1.0 - lb) * jax.nn.sigmoid(f_ref[r0:r0 + c, :]))
    g = jnp.log(fg)
    g_hi = g.astype(BF16)
    g_lo = (g - g_hi.astype(F32)).astype(BF16)
    tril = wall[0:c]
    b = (jnp.dot(tril, g_hi, preferred_element_type=F32)
         + jnp.dot(tril, g_lo, preferred_element_type=F32))
    safe = jnp.min(b[c - 1:c, :]) >= -HG_SAFE_EXP
    eb = jnp.exp(b)
    eb_last = eb[c - 1:c, :]
    qb = (side_by_side(lambda r0: _silu(q_ref[r0:r0 + c, :])) * eb).astype(BF16)
    ke = (1.0 - fg) * jnp.exp(-b)
    keb = ke.astype(BF16)
    kd = (ke * eb_last).astype(BF16)
    vbs = [i_ref[n * c:(n + 1) * c, :].astype(BF16) for n in range(n_chunks)]
    lanes_of = lambda x, n: x[:, n * dk:(n + 1) * dk]
    causal = r2 >= c2
    a = [jnp.where(causal, lax.dot_general(lanes_of(qb, n), lanes_of(keb, n), nt, preferred_element_type=F32),
                   0.0).astype(BF16) for n in range(n_chunks)]
    o_intra = [jnp.dot(a[n], vbs[n], preferred_element_type=F32) for n in range(n_chunks)]
    upd = [lax.dot_general(vbs[n], lanes_of(kd, n), tn, preferred_element_type=F32) for n in range(n_chunks)]
    st = st_ref[...]
    for n in range(n_chunks):
        o_ref[n * c:(n + 1) * c, :] = o_intra[n] + lax.dot_general(
            lanes_of(qb, n), st.astype(BF16), nt, preferred_element_type=F32)
        st = lanes_of(eb_last, n) * st + upd[n]
    st_ref[...] = st

    @pl.when(jnp.logical_not(safe))
    def _():
        st_ref[...] = st0_ref[...]
        lax.fori_loop(0, n_chunks, chunk, 0, unroll=2)


def _hgrn_core(u, lb, *, tt=1024):
    s = u.shape[0]
    nh, dk = HG_HEADS, HG_DIM
    wall = jnp.asarray(_hgrn_exponent_matrix(), BF16)
    return pl.pallas_call(
        functools.partial(_hgrn_kernel, tt=tt),
        grid=(nh, s // tt),
        in_specs=[
            pl.BlockSpec((tt, dk), lambda h, t: (t, h)),
            pl.BlockSpec((tt, dk), lambda h, t: (t, nh + h)),
            pl.BlockSpec((tt, dk), lambda h, t: (t, 2 * nh + h)),
            pl.BlockSpec((1, dk), lambda h, t: (0, h)),
            pl.BlockSpec(wall.shape, lambda h, t: (0, 0)),
        ],
        out_specs=pl.BlockSpec((tt, dk), lambda h, t: (t, h)),
        out_shape=jax.ShapeDtypeStruct((s, nh * dk), F32),
        scratch_shapes=[pltpu.VMEM((dk, dk), F32), pltpu.VMEM((dk, dk), F32)],
        compiler_params=_cparams("arbitrary", "arbitrary"),
        name="hgrn2_core",
    )(u, u, u, lb, wall)


def _conv_kernel(a_ref, ag_ref, gate_ref, dw_ref, dwb_ref, lng_ref, lnb_ref, o_ref, ybuf, cbuf, shbuf,
                 *, tt, cw):
    halo = CONV_HALO
    d = a_ref.shape[1]

    @pl.when(pl.program_id(0) == 0)
    def _():
        ybuf[0:halo, :] = jnp.zeros((halo, d), F32)

    @pl.when(pl.program_id(0) > 0)
    def _():
        ybuf[0:halo, :] = ybuf[tt:tt + halo, :]

    strip = 16

    def glu(si, carry):
        r = pl.multiple_of(si * strip, strip)
        ybuf[pl.ds(pl.multiple_of(halo + r, strip), strip), :] = (
            a_ref[pl.ds(r, strip), :] * _sigmoid(ag_ref[pl.ds(r, strip), :]))
        return carry

    lax.fori_loop(0, tt // strip, glu, 0)
    first = halo - (CONV_WIDTH - 1)

    def lanes(ci, carry):
        c0 = pl.multiple_of(ci * cw, cw)
        acc = jnp.broadcast_to(dwb_ref[:, pl.ds(c0, cw)], (tt, cw))
        for rho in range(8):
            taps = [k for k in range(CONV_WIDTH) if (first + k) % 8 == rho]
            if not taps:
                continue
            span = tt + 8 * max((first + k) // 8 for k in taps)
            if rho:
                shbuf[rho, 0:span, :] = ybuf[rho:rho + span, pl.ds(c0, cw)]
            for k in taps:
                a8 = 8 * ((first + k) // 8)
                yk = shbuf[rho, a8:a8 + tt, :] if rho else ybuf[a8:a8 + tt, pl.ds(c0, cw)]
                acc = acc + yk * dw_ref[k:k + 1, pl.ds(c0, cw)]
        cbuf[:, pl.ds(c0, cw)] = acc
        return carry

    lax.fori_loop(0, d // cw, lanes, 0)

    def layernorm_gate(si, carry):
        r = pl.multiple_of(si * strip, strip)
        y = cbuf[pl.ds(r, strip), :]
        mu = jnp.mean(y, axis=-1, keepdims=True)
        yc = y - mu
        var = jnp.mean(yc * yc, axis=-1, keepdims=True)
        z = yc * lax.rsqrt(var + EPS) * lng_ref[...] + lnb_ref[...]
        o_ref[pl.ds(r, strip), :] = (_silu(z) * _silu(gate_ref[pl.ds(r, strip), :])).astype(o_ref.dtype)
        return carry

    lax.fori_loop(0, tt // strip, layernorm_gate, 0, unroll=2)


def _conv_core(u, dw, dw_b, ln_g, ln_b, *, tt=256, cw=128):
    s = u.shape[0]
    d = dw.shape[1]
    row = pl.BlockSpec((1, d), lambda t: (0, 0))
    return pl.pallas_call(
        functools.partial(_conv_kernel, tt=tt, cw=cw),
        grid=(s // tt,),
        in_specs=[
            pl.BlockSpec((tt, d), lambda t: (t, 0)),
            pl.BlockSpec((tt, d), lambda t: (t, 1)),
            pl.BlockSpec((tt, d), lambda t: (t, 2)),
            pl.BlockSpec(dw.shape, lambda t: (0, 0)),
            row, row, row,
        ],
        out_specs=pl.BlockSpec((tt, d), lambda t: (t, 0)),
        out_shape=jax.ShapeDtypeStruct((s, d), BF16),
        scratch_shapes=[pltpu.VMEM((tt + CONV_HALO, d), F32), pltpu.VMEM((tt, d), F32),
                        pltpu.VMEM((8, tt + CONV_HALO, cw), F32)],
        compiler_params=_cparams("arbitrary"),
        name="conv_module_core",
    )(u, u, u, dw, dw_b, ln_g, ln_b)


def kernel(x, c, norm_g, ada_w, ada_b, attn_w_in, attn_w_out, attn_lam_q1, attn_lam_k1, attn_lam_q2, attn_lam_k2, attn_subln_g, hgrn_w_in, hgrn_w_out, hgrn_lb_logits, hgrn_gnorm_g, conv_w_in, conv_b_in, conv_dw, conv_dw_b, conv_ln_g, conv_ln_b, conv_w_out, conv_b_out, final_g):
    bsz, s, d = x.shape
    depth = norm_g.shape[0]
    assert bsz == 1 and c.shape == (1, d)
    xs = x.reshape(s, d)
    mods = _mods(c, ada_w, ada_b)

    slopes = (2.0 ** (-8.0 * jnp.arange(1, DA_HEADS + 1, dtype=F32) / DA_HEADS)) * LOG2E
    qk_cols = 2 * DA_HEADS * DA_QK_DIM
    attn_colscale = jnp.concatenate(
        [jnp.full((1, qk_cols), DA_QK_DIM ** -0.5 * LOG2E, F32), jnp.ones((1, 3 * qk_cols), F32)], axis=1)
    lb_all = jax.nn.softmax(hgrn_lb_logits.astype(F32), axis=0)
    lb_all = jnp.cumsum(lb_all, axis=0) - lb_all[0]

    def mod_rows(i):
        return tuple(mods[i, :, k * d:(k + 1) * d] for k in range(3))

    shift, scale, gate = mod_rows(0)
    h = _norm(xs, norm_g[0:1], scale, shift, modulate=True, out_dtype=BF16)
    for i in range(depth):
        kind, j = i % N_MIXERS, i // N_MIXERS
        bias_out = pre = None
        if kind == 0:
            u = _matmul(h, attn_w_in, j, out_dtype=BF16, colscale=attn_colscale)
            lam_init = 0.8 - 0.6 * math.exp(-0.3 * i)
            lam = (jnp.exp(jnp.sum(attn_lam_q1[j] * attn_lam_k1[j]))
                   - jnp.exp(jnp.sum(attn_lam_q2[j] * attn_lam_k2[j])) + lam_init).reshape(1)
            y = _attention(u, slopes, lam, attn_subln_g[j:j + 1], out_scale=1.0 - lam_init)
            w_out = attn_w_out[j]
        elif kind == 1:
            u = _matmul(h, hgrn_w_in, j, out_dtype=F32)
            y = _hgrn_core(u, lb_all[i:i + 1])
            pre = (u, u.shape[1] // d - 1, hgrn_gnorm_g[j:j + 1])
            w_out = hgrn_w_out[j]
        else:
            u = _matmul(h, conv_w_in, j, out_dtype=F32, bias=conv_b_in[j:j + 1])
            y = _conv_core(u, conv_dw[j], conv_dw_b[j:j + 1], conv_ln_g[j:j + 1], conv_ln_b[j:j + 1])
            w_out = conv_w_out[j]
            bias_out = conv_b_out[j:j + 1]
        if i + 1 < depth:
            shift_n, scale_n, gate_n = mod_rows(i + 1)
            xs, h = _outproj(y, w_out.astype(BF16), xs, gate, bias_out, norm_g[i + 1:i + 2], scale_n, shift_n,
                             modulate=True, write_x=True, out_dtype=BF16, pre=pre)
            gate = gate_n
        else:
            zero = jnp.zeros((1, d), F32)
            out, = _outproj(y, w_out.astype(BF16), xs, gate, bias_out, final_g.reshape(1, d), zero, zero,
                            modulate=False, write_x=False, out_dtype=F32, pre=pre)
    return out.reshape(bsz, s, d)
```

```python
import functools
import math

import numpy as np
import jax
import jax.numpy as jnp
from jax import lax
from jax.experimental import pallas as pl
from jax.experimental.pallas import tpu as pltpu

F32 = jnp.float32
BF16 = jnp.bfloat16

EPS = 1e-6
LOG2E = 1.4426950408889634
N_MIXERS = 3

DA_HEADS = 8
DA_QK_DIM = 128
DA_V_DIM = 256
ATT_SKIP_LOG2 = 160.0
ATT_NORM_MARGIN = 1.01
HG_HEADS = 16
HG_DIM = 128
HG_CHUNK = 64
HG_LEVELS = (32, 16, 8)
HG_SAFE_EXP = 80.0
CONV_WIDTH = 31
CONV_HALO = 32

VMEM_LIMIT_BYTES = 56 * 1024 * 1024


def _cparams(*sem):
    return pltpu.CompilerParams(dimension_semantics=sem, vmem_limit_bytes=VMEM_LIMIT_BYTES)


def _sigmoid(x):
    return 0.5 * jnp.tanh(0.5 * x) + 0.5


def _silu(x):
    h = 0.5 * x
    return h + h * jnp.tanh(h)


def _mods_kernel(c_ref, w_ref, b_ref, o_ref, ca_ref, acc_ref, *, kc, ln):
    kstep = pl.program_id(1)
    kb, n = w_ref.shape[1], w_ref.shape[2]
    ca_ref[...] = _silu(c_ref[...])

    @pl.when(kstep == 0)
    def _():
        acc_ref[...] = jnp.zeros_like(acc_ref)

    for lb in range(n // ln):
        cols = slice(lb * ln, (lb + 1) * ln)

        def body(i, acc):
            k0 = pl.multiple_of(i * kc, kc)
            prod = w_ref[0, pl.ds(k0, kc), cols] * ca_ref[pl.ds(k0, kc), :]
            return acc + prod.reshape(kc // 8, 8, ln).sum(axis=0)

        acc_ref[:, cols] = lax.fori_loop(0, kb // kc, body, acc_ref[:, cols], unroll=4)

    @pl.when(kstep == pl.num_programs(1) - 1)
    def _():
        o_ref[0] = jnp.sum(acc_ref[...], axis=0, keepdims=True) + b_ref[0]


def _mods(c, ada_w, ada_b, *, kb=512, kc=32, ln=1024):
    depth, d, n = ada_w.shape
    return pl.pallas_call(
        functools.partial(_mods_kernel, kc=kc, ln=ln),
        grid=(depth, d // kb),
        in_specs=[
            pl.BlockSpec((kb, 1), lambda i, k: (k, 0)),
            pl.BlockSpec((1, kb, n), lambda i, k: (i, k, 0)),
            pl.BlockSpec((1, 1, n), lambda i, k: (i, 0, 0)),
        ],
        out_specs=pl.BlockSpec((1, 1, n), lambda i, k: (i, 0, 0)),
        out_shape=jax.ShapeDtypeStruct((depth, 1, n), F32),
        scratch_shapes=[pltpu.VMEM((kb, 1), F32), pltpu.VMEM((8, n), F32)],
        compiler_params=_cparams("arbitrary", "arbitrary"),
        name="adaln_mods",
    )(c.reshape(d, 1), ada_w, ada_b.reshape(depth, 1, n))


def _norm_kernel(x_ref, g_ref, sc_ref, sh_ref, o_ref, *, modulate):
    x = x_ref[...]
    y = x * lax.rsqrt(jnp.mean(x * x, axis=-1, keepdims=True) + EPS) * g_ref[...]
    if modulate:
        y = y * (1.0 + sc_ref[...]) + sh_ref[...]
    o_ref[...] = y.astype(o_ref.dtype)


def _norm(x, gain, scale, shift, *, modulate, out_dtype, tm=512):
    s, d = x.shape
    row = pl.BlockSpec((1, d), lambda i: (0, 0))
    return pl.pallas_call(
        functools.partial(_norm_kernel, modulate=modulate),
        grid=(s // tm,),
        in_specs=[pl.BlockSpec((tm, d), lambda i: (i, 0)), row, row, row],
        out_specs=pl.BlockSpec((tm, d), lambda i: (i, 0)),
        out_shape=jax.ShapeDtypeStruct((s, d), out_dtype),
        compiler_params=_cparams("arbitrary"),
        name="rmsnorm_mod",
    )(x, gain, scale, shift)


def _mm_kernel(*refs, has_bias, has_colscale):
    a_ref, w_ref = refs[0], refs[1]
    o_ref, wb_ref = refs[-2], refs[-1]
    pos = 2

    @pl.when(pl.program_id(1) == 0)
    def _():
        wb_ref[...] = w_ref[...].astype(BF16)

    acc = jnp.dot(a_ref[...], wb_ref[...], preferred_element_type=F32)
    if has_bias:
        acc = acc + refs[pos][...]
        pos += 1
    if has_colscale:
        acc = acc * refs[pos][...]
    o_ref[...] = acc.astype(o_ref.dtype)


def _matmul(a, w, layer, *, out_dtype, bias=None, colscale=None, tm=1024, tn=1024):
    m, k = a.shape
    n_out = w.shape[2]
    in_specs = [
        pl.BlockSpec((tm, k), lambda j, i: (i, 0)),
        pl.BlockSpec((None, k, tn), lambda j, i: (layer, 0, j)),
    ]
    args = [a, w]
    rowspec = pl.BlockSpec((1, tn), lambda j, i: (0, j))
    for extra in (bias, colscale):
        if extra is not None:
            in_specs.append(rowspec)
            args.append(extra)
    return pl.pallas_call(
        functools.partial(_mm_kernel, has_bias=bias is not None, has_colscale=colscale is not None),
        grid=(n_out // tn, m // tm),
        in_specs=in_specs,
        out_specs=pl.BlockSpec((tm, tn), lambda j, i: (i, j)),
        out_shape=jax.ShapeDtypeStruct((m, n_out), out_dtype),
        scratch_shapes=[pltpu.VMEM((k, tn), BF16)],
        compiler_params=_cparams("arbitrary", "arbitrary"),
        name="matmul_in",
    )(*args)


def _outproj_kernel(*refs, has_bias, has_pre, modulate, write_x):
    a_ref, w_ref, x_ref, gate_ref = refs[:4]
    pos = 4
    a = a_ref[...]
    if has_pre:
        a = a * lax.rsqrt(jnp.mean(a * a, axis=-1, keepdims=True) + EPS) * refs[pos + 1][...]
        a = (a * _silu(refs[pos][...])).astype(BF16)
        pos += 2
    y = jnp.dot(a, w_ref[...], preferred_element_type=F32)
    if has_bias:
        y = y + refs[pos][...]
        pos += 1
    g_ref, sc_ref, sh_ref = refs[pos:pos + 3]
    xn = x_ref[...] + gate_ref[...] * y
    if write_x:
        refs[-2][...] = xn
    hn = xn * lax.rsqrt(jnp.mean(xn * xn, axis=-1, keepdims=True) + EPS) * g_ref[...]
    if modulate:
        hn = hn * (1.0 + sc_ref[...]) + sh_ref[...]
    refs[-1][...] = hn.astype(refs[-1].dtype)


def _outproj(a, w, x, gate, bias, gain, scale, shift, *, modulate, write_x, out_dtype, pre=None, tm=None):
    m, k = a.shape
    d = w.shape[1]
    if tm is None:
        tm = 256 if pre is not None else 512
    row = pl.BlockSpec((1, d), lambda i: (0, 0))
    tile = pl.BlockSpec((tm, d), lambda i: (i, 0))
    in_specs = [pl.BlockSpec((tm, k), lambda i: (i, 0)), pl.BlockSpec((k, d), lambda i: (0, 0)), tile, row]
    args = [a, w, x, gate]
    if pre is not None:
        u, col_block, pre_gain = pre
        in_specs += [pl.BlockSpec((tm, k), lambda i: (i, col_block)), pl.BlockSpec((1, k), lambda i: (0, 0))]
        args += [u, pre_gain]
    if bias is not None:
        in_specs.append(row)
        args.append(bias)
    in_specs += [row, row, row]
    args += [gain, scale, shift]
    out_specs = [tile]
    out_shape = [jax.ShapeDtypeStruct((m, d), out_dtype)]
    if write_x:
        out_specs = [tile, tile]
        out_shape = [jax.ShapeDtypeStruct((m, d), F32)] + out_shape
    return pl.pallas_call(
        functools.partial(_outproj_kernel, has_bias=bias is not None, has_pre=pre is not None,
                          modulate=modulate, write_x=write_x),
        grid=(m // tm,),
        in_specs=in_specs,
        out_specs=out_specs,
        out_shape=out_shape,
        compiler_params=_cparams("arbitrary"),
        name="matmul_out",
    )(*args)


def _attn_kernel(slope_ref, lam_ref, q_ref, k_ref, v_ref, g_ref, sg_ref, o_ref, acc_ref, l_ref, s_ref,
                 kn_ref, *, tq, out_scale):
    h = pl.program_id(0)
    i = pl.program_id(1)
    dk = DA_QK_DIM
    slope = slope_ref[h]
    lam = lam_ref[0]
    lanes = 128
    q0 = i * tq

    def scores(mi, k0, tk, m_old, diag_col):
        colf = lax.broadcasted_iota(jnp.int32, (1, tk), 1).astype(F32)
        bias = slope * (colf + (k0 - q0).astype(F32))
        qm = q_ref[:, mi * dk:(mi + 1) * dk]
        km = k_ref[pl.ds(k0, tk), mi * dk:(mi + 1) * dk]
        s = lax.dot_general(qm, km, (((1,), (1,)), ((), ())), preferred_element_type=F32) + bias
        if diag_col is not None:
            r = lax.broadcasted_iota(jnp.int32, (tq, tk), 0)
            c = lax.broadcasted_iota(jnp.int32, (tq, tk), 1)
            s = jnp.where(r + diag_col >= c, s, -jnp.inf)
        s_ref[mi, :, 0:tk] = s
        smax = jnp.max(s, axis=-1, keepdims=True)
        return smax if m_old is None else jnp.maximum(m_old, smax)

    def accumulate(mi, k0, tk, m_old, m_new):
        vb = v_ref[pl.ds(k0, tk), :]
        p = jnp.exp2(s_ref[mi, :, 0:tk] - m_new)
        psum = p[:, 0:lanes]
        for t in range(1, tk // lanes):
            psum = psum + p[:, t * lanes:(t + 1) * lanes]
        if m_old is None:
            l_ref[mi] = psum
            acc_ref[mi] = jnp.dot(p.astype(BF16), vb, preferred_element_type=F32)
        else:
            alpha = jnp.exp2(m_old - m_new)
            l_ref[mi] = alpha * l_ref[mi] + psum
            acc_ref[mi] = alpha * acc_ref[mi] + jnp.dot(p.astype(BF16), vb, preferred_element_type=F32)

    def tile(k0, tk, m, diag_col=None):
        k0 = pl.multiple_of(k0, tq)
        old = (None, None) if m is None else m
        new = tuple(scores(mi, k0, tk, old[mi], diag_col) for mi in range(2))
        for mi in range(2):
            accumulate(mi, k0, tk, old[mi], new[mi])
        return new

    odd = i % 2 == 1
    m = lax.cond(odd, lambda: tile(q0 - tq, 2 * tq, None, diag_col=tq), lambda: tile(q0, tq, None, diag_col=0))

    @pl.when(i == 0)
    def _():
        for blk in range(k_ref.shape[0] // tq):
            for mi in range(2):
                x = k_ref[blk * tq:(blk + 1) * tq, mi * dk:(mi + 1) * dk].astype(F32)
                kn_ref[mi, blk] = jnp.sqrt(jnp.max(jnp.sum(x * x, axis=-1, keepdims=True)))

    qn, floor = [], []
    for mi in range(2):
        x = q_ref[:, mi * dk:(mi + 1) * dk].astype(F32)
        qn.append(jnp.sqrt(jnp.max(jnp.sum(x * x, axis=-1, keepdims=True))) * ATT_NORM_MARGIN)
        floor.append(jnp.min(m[mi]) - ATT_SKIP_LOG2)

    def dead(k0, tk):
        b0 = k0 // tq
        top_bias = slope * (k0 + (tk - 1) - q0).astype(F32)
        ok = None
        for mi in range(2):
            kmax = kn_ref[mi, b0]
            for t in range(1, tk // tq):
                kmax = jnp.maximum(kmax, kn_ref[mi, b0 + t])
            below = qn[mi] * kmax + top_bias < floor[mi]
            ok = below if ok is None else jnp.logical_and(ok, below)
        return ok

    def visit(k0, w, m, take=True):
        half = w // 2
        near_only = functools.partial(lambda m, k0, half: tile(k0 + half, half, m), k0=k0, half=half)
        whole = functools.partial(lambda m, k0, w: tile(k0, w, m), k0=k0, w=w)
        live = jnp.logical_and(take, jnp.logical_not(dead(k0, w)))
        return lax.cond(live, lambda m: lax.cond(dead(k0, half), near_only, whole, m), lambda m: m, m)

    wide = s_ref.shape[2]
    w = 2 * tq
    end = q0 - jnp.where(odd, tq, 0)
    while w < wide:
        take = (i // (w // tq)) % 2 == 1
        m = visit(jnp.where(take, end - w, 0), w, m, take)
        end = end - jnp.where(take, w, 0)
        w *= 2
    lax.fori_loop(0, i // (wide // tq), lambda jj, m: visit(jj * wide, wide, m), m)

    inv1 = 1.0 / jnp.sum(l_ref[0], axis=-1, keepdims=True)
    inv2 = lam / jnp.sum(l_ref[1], axis=-1, keepdims=True)
    o = acc_ref[0] * inv1 - acc_ref[1] * inv2
    y = o * lax.rsqrt(jnp.mean(o * o, axis=-1, keepdims=True) + EPS) * sg_ref[...] * out_scale
    o_ref[...] = (y * _silu(g_ref[...].astype(F32))).astype(o_ref.dtype)


def _attention(u, slopes, lam, subln_g, *, out_scale, tq=512, tk_wide=2048):
    s = u.shape[0]
    dv = DA_V_DIM
    nh = DA_HEADS
    smem = pl.BlockSpec(memory_space=pltpu.SMEM)
    return pl.pallas_call(
        functools.partial(_attn_kernel, tq=tq, out_scale=out_scale),
        grid=(nh, s // tq),
        in_specs=[
            smem, smem,
            pl.BlockSpec((tq, dv), lambda h, i: (i, h)),
            pl.BlockSpec((s, dv), lambda h, i: (0, nh + h)),
            pl.BlockSpec((s, dv), lambda h, i: (0, 2 * nh + h)),
            pl.BlockSpec((tq, dv), lambda h, i: (i, 3 * nh + h)),
            pl.BlockSpec((1, dv), lambda h, i: (0, 0)),
        ],
        out_specs=pl.BlockSpec((tq, dv), lambda h, i: (i, h)),
        out_shape=jax.ShapeDtypeStruct((s, nh * dv), BF16),
        scratch_shapes=[pltpu.VMEM((2, tq, dv), F32), pltpu.VMEM((2, tq, 128), F32),
                        pltpu.VMEM((2, tq, tk_wide), F32), pltpu.SMEM((2, s // tq), F32)],
        compiler_params=_cparams("arbitrary", "arbitrary"),
        name="diff_attention",
    )(slopes, lam, u, u, u, u, subln_g)


def _hgrn_exponent_matrix():
    c = HG_CHUNK
    w = np.zeros((len(HG_LEVELS) + 1, c, c), np.float32)
    for r in range(c):
        w[0, r, :r + 1] = 1.0
        for li, m in enumerate(HG_LEVELS):
            mid = (r // (2 * m)) * 2 * m + m
            if r % (2 * m) >= m:
                w[li + 1, r, mid:r + 1] = 1.0
            else:
                w[li + 1, r, r + 1:mid] = 1.0
    return w.reshape(-1, c)


def _hgrn_kernel(q_ref, f_ref, i_ref, lb_ref, w_ref, o_ref, st_ref, st0_ref, *, tt):
    c = HG_CHUNK

    @pl.when(pl.program_id(1) == 0)
    def _():
        st_ref[...] = jnp.zeros_like(st_ref)

    lb = lb_ref[...]
    wall = w_ref[...]
    row = lax.broadcasted_iota(jnp.int32, (c, 1), 0)
    r2 = lax.broadcasted_iota(jnp.int32, (c, c), 0)
    c2 = lax.broadcasted_iota(jnp.int32, (c, c), 1)
    row8 = lax.broadcasted_iota(jnp.int32, (8, 1), 0)
    nt = (((1,), (1,)), ((), ()))
    tn = (((0,), (0,)), ((), ()))

    def chunk(ci, carry):
        r0 = pl.multiple_of(ci * c, c)
        q = _silu(q_ref[pl.ds(r0, c), :])
        v = i_ref[pl.ds(r0, c), :]
        fg = lb + (1.0 - lb) * jax.nn.sigmoid(f_ref[pl.ds(r0, c), :])
        kk = 1.0 - fg
        g = jnp.log(fg)
        g_hi = g.astype(BF16)
        g_lo = (g - g_hi.astype(F32)).astype(BF16)
        ex = (jnp.dot(wall, g_hi, preferred_element_type=F32)
              + jnp.dot(wall, g_lo, preferred_element_type=F32))
        b = ex[0:c]
        st = st_ref[...]
        vb = v.astype(BF16)

        o = lax.dot_general((q * jnp.exp(b)).astype(BF16), st.astype(BF16), nt,
                            preferred_element_type=F32)

        a = jnp.zeros((c, c), F32)
        for li, m in enumerate(HG_LEVELS):
            e = jnp.exp(ex[(li + 1) * c:(li + 2) * c])
            upper = (row % (2 * m)) >= m
            qt = jnp.where(upper, q * e, 0.0).astype(BF16)
            kt = jnp.where(upper, 0.0, kk * e).astype(BF16)
            al = lax.dot_general(qt, kt, nt, preferred_element_type=F32)
            if 2 * m != c:
                al = jnp.where((r2 // (2 * m)) == (c2 // (2 * m)), al, 0.0)
            a = a + al
        o = o + jnp.dot(a.astype(BF16), vb, preferred_element_type=F32)

        parts = []
        for blk in range(c // 8):
            sl = slice(8 * blk, 8 * blk + 8)
            bb, kb, v8, q8 = b[sl], kk[sl], v[sl], q[sl]
            od = jnp.zeros((8, HG_DIM), F32)
            for s in range(8):
                e = jnp.exp(jnp.where(row8 >= s, bb - bb[s:s + 1], -jnp.inf))
                w = jnp.sum(e * q8 * kb[s:s + 1], axis=-1, keepdims=True)
                od = od + w * v8[s:s + 1]
            parts.append(od)
        o_ref[pl.ds(r0, c), :] = o + jnp.concatenate(parts, axis=0)

        b_last = b[c - 1:c]
        kd = (kk * jnp.exp(b_last - b)).astype(BF16)
        st_ref[...] = jnp.exp(b_last) * st + lax.dot_general(vb, kd, tn, preferred_element_type=F32)
        return carry

    n_chunks = tt // c
    dk = HG_DIM
    st0_ref[...] = st_ref[...]

    def side_by_side(fn):
        return jnp.concatenate([fn(n * c) for n in range(n_chunks)], axis=1)

    fg = side_by_side(lambda r0: lb + (1.0 - lb) * jax.nn.sigmoid(f_ref[r0:r0 + c, :]))
    g = jnp.log(fg)
    g_hi = g.astype(BF16)
    g_lo = (g - g_hi.astype(F32)).astype(BF16)
    tril = wall[0:c]
    b = (jnp.dot(tril, g_hi, preferred_element_type=F32)
         + jnp.dot(tril, g_lo, preferred_element_type=F32))
    safe = jnp.min(b[c - 1:c, :]) >= -HG_SAFE_EXP
    eb = jnp.exp(b)
    eb_last = eb[c - 1:c, :]
    qb = (side_by_side(lambda r0: _silu(q_ref[r0:r0 + c, :])) * eb).astype(BF16)
    ke = (1.0 - fg) * jnp.exp(-b)
    keb = ke.astype(BF16)
    kd = (ke * eb_last).astype(BF16)
    vbs = [i_ref[n * c:(n + 1) * c, :].astype(BF16) for n in range(n_chunks)]
    lanes_of = lambda x, n: x[:, n * dk:(n + 1) * dk]
    causal = r2 >= c2
    a = [jnp.where(causal, lax.dot_general(lanes_of(qb, n), lanes_of(keb, n), nt, preferred_element_type=F32),
                   0.0).astype(BF16) for n in range(n_chunks)]
    o_intra = [jnp.dot(a[n], vbs[n], preferred_element_type=F32) for n in range(n_chunks)]
    upd = [lax.dot_general(vbs[n], lanes_of(kd, n), tn, preferred_element_type=F32) for n in range(n_chunks)]
    st = st_ref[...]
    for n in range(n_chunks):
        o_ref[n * c:(n + 1) * c, :] = o_intra[n] + lax.dot_general(
            lanes_of(qb, n), st.astype(BF16), nt, preferred_element_type=F32)
        st = lanes_of(eb_last, n) * st + upd[n]
    st_ref[...] = st

    @pl.when(jnp.logical_not(safe))
    def _():
        st_ref[...] = st0_ref[...]
        lax.fori_loop(0, n_chunks, chunk, 0, unroll=2)


def _hgrn_core(u, lb, *, tt=1024):
    s = u.shape[0]
    nh, dk = HG_HEADS, HG_DIM
    wall = jnp.asarray(_hgrn_exponent_matrix(), BF16)
    return pl.pallas_call(
        functools.partial(_hgrn_kernel, tt=tt),
        grid=(nh, s // tt),
        in_specs=[
            pl.BlockSpec((tt, dk), lambda h, t: (t, h)),
            pl.BlockSpec((tt, dk), lambda h, t: (t, nh + h)),
            pl.BlockSpec((tt, dk), lambda h, t: (t, 2 * nh + h)),
            pl.BlockSpec((1, dk), lambda h, t: (0, h)),
            pl.BlockSpec(wall.shape, lambda h, t: (0, 0)),
        ],
        out_specs=pl.BlockSpec((tt, dk), lambda h, t: (t, h)),
        out_shape=jax.ShapeDtypeStruct((s, nh * dk), F32),
        scratch_shapes=[pltpu.VMEM((dk, dk), F32), pltpu.VMEM((dk, dk), F32)],
        compiler_params=_cparams("arbitrary", "arbitrary"),
        name="hgrn2_core",
    )(u, u, u, lb, wall)


def _conv_kernel(a_ref, ag_ref, gate_ref, dw_ref, dwb_ref, lng_ref, lnb_ref, o_ref, ybuf, cbuf, shbuf,
                 *, tt, cw):
    halo = CONV_HALO
    d = a_ref.shape[1]

    @pl.when(pl.program_id(0) == 0)
    def _():
        ybuf[0:halo, :] = jnp.zeros((halo, d), F32)

    @pl.when(pl.program_id(0) > 0)
    def _():
        ybuf[0:halo, :] = ybuf[tt:tt + halo, :]

    strip = 16

    def glu(si, carry):
        r = pl.multiple_of(si * strip, strip)
        ybuf[pl.ds(pl.multiple_of(halo + r, strip), strip), :] = (
            a_ref[pl.ds(r, strip), :] * _sigmoid(ag_ref[pl.ds(r, strip), :]))
        return carry

    lax.fori_loop(0, tt // strip, glu, 0)
    first = halo - (CONV_WIDTH - 1)

    def lanes(ci, carry):
        c0 = pl.multiple_of(ci * cw, cw)
        acc = jnp.broadcast_to(dwb_ref[:, pl.ds(c0, cw)], (tt, cw))
        for rho in range(8):
            taps = [k for k in range(CONV_WIDTH) if (first + k) % 8 == rho]
            if not taps:
                continue
            span = tt + 8 * max((first + k) // 8 for k in taps)
            if rho:
                shbuf[rho, 0:span, :] = ybuf[rho:rho + span, pl.ds(c0, cw)]
            for k in taps:
                a8 = 8 * ((first + k) // 8)
                yk = shbuf[rho, a8:a8 + tt, :] if rho else ybuf[a8:a8 + tt, pl.ds(c0, cw)]
                acc = acc + yk * dw_ref[k:k + 1, pl.ds(c0, cw)]
        cbuf[:, pl.ds(c0, cw)] = acc
        return carry

    lax.fori_loop(0, d // cw, lanes, 0)

    def layernorm_gate(si, carry):
        r = pl.multiple_of(si * strip, strip)
        y = cbuf[pl.ds(r, strip), :]
        mu = jnp.mean(y, axis=-1, keepdims=True)
        yc = y - mu
        var = jnp.mean(yc * yc, axis=-1, keepdims=True)
        z = yc * lax.rsqrt(var + EPS) * lng_ref[...] + lnb_ref[...]
        o_ref[pl.ds(r, strip), :] = (_silu(z) * _silu(gate_ref[pl.ds(r, strip), :])).astype(o_ref.dtype)
        return carry

    lax.fori_loop(0, tt // strip, layernorm_gate, 0, unroll=4)


def _conv_core(u, dw, dw_b, ln_g, ln_b, *, tt=256, cw=128):
    s = u.shape[0]
    d = dw.shape[1]
    row = pl.BlockSpec((1, d), lambda t: (0, 0))
    return pl.pallas_call(
        functools.partial(_conv_kernel, tt=tt, cw=cw),
        grid=(s // tt,),
        in_specs=[
            pl.BlockSpec((tt, d), lambda t: (t, 0)),
            pl.BlockSpec((tt, d), lambda t: (t, 1)),
            pl.BlockSpec((tt, d), lambda t: (t, 2)),
            pl.BlockSpec(dw.shape, lambda t: (0, 0)),
            row, row, row,
        ],
        out_specs=pl.BlockSpec((tt, d), lambda t: (t, 0)),
        out_shape=jax.ShapeDtypeStruct((s, d), BF16),
        scratch_shapes=[pltpu.VMEM((tt + CONV_HALO, d), F32), pltpu.VMEM((tt, d), F32),
                        pltpu.VMEM((8, tt + CONV_HALO, cw), F32)],
        compiler_params=_cparams("arbitrary"),
        name="conv_module_core",
    )(u, u, u, dw, dw_b, ln_g, ln_b)


def kernel(x, c, norm_g, ada_w, ada_b, attn_w_in, attn_w_out, attn_lam_q1, attn_lam_k1, attn_lam_q2, attn_lam_k2, attn_subln_g, hgrn_w_in, hgrn_w_out, hgrn_lb_logits, hgrn_gnorm_g, conv_w_in, conv_b_in, conv_dw, conv_dw_b, conv_ln_g, conv_ln_b, conv_w_out, conv_b_out, final_g):
    bsz, s, d = x.shape
    depth = norm_g.shape[0]
    assert bsz == 1 and c.shape == (1, d)
    xs = x.reshape(s, d)
    mods = _mods(c, ada_w, ada_b)

    slopes = (2.0 ** (-8.0 * jnp.arange(1, DA_HEADS + 1, dtype=F32) / DA_HEADS)) * LOG2E
    qk_cols = 2 * DA_HEADS * DA_QK_DIM
    attn_colscale = jnp.concatenate(
        [jnp.full((1, qk_cols), DA_QK_DIM ** -0.5 * LOG2E, F32), jnp.ones((1, 3 * qk_cols), F32)], axis=1)
    lb_all = jax.nn.softmax(hgrn_lb_logits.astype(F32), axis=0)
    lb_all = jnp.cumsum(lb_all, axis=0) - lb_all[0]

    def mod_rows(i):
        return tuple(mods[i, :, k * d:(k + 1) * d] for k in range(3))

    shift, scale, gate = mod_rows(0)
    h = _norm(xs, norm_g[0:1], scale, shift, modulate=True, out_dtype=BF16)
    for i in range(depth):
        kind, j = i % N_MIXERS, i // N_MIXERS
        bias_out = pre = None
        if kind == 0:
            u = _matmul(h, attn_w_in, j, out_dtype=BF16, colscale=attn_colscale)
            lam_init = 0.8 - 0.6 * math.exp(-0.3 * i)
            lam = (jnp.exp(jnp.sum(attn_lam_q1[j] * attn_lam_k1[j]))
                   - jnp.exp(jnp.sum(attn_lam_q2[j] * attn_lam_k2[j])) + lam_init).reshape(1)
            y = _attention(u, slopes, lam, attn_subln_g[j:j + 1], out_scale=1.0 - lam_init)
            w_out = attn_w_out[j]
        elif kind == 1:
            u = _matmul(h, hgrn_w_in, j, out_dtype=F32)
            y = _hgrn_core(u, lb_all[i:i + 1])
            pre = (u, u.shape[1] // d - 1, hgrn_gnorm_g[j:j + 1])
            w_out = hgrn_w_out[j]
        else:
            u = _matmul(h, conv_w_in, j, out_dtype=F32, bias=conv_b_in[j:j + 1])
            y = _conv_core(u, conv_dw[j], conv_dw_b[j:j + 1], conv_ln_g[j:j + 1], conv_ln_b[j:j + 1])
            w_out = conv_w_out[j]
            bias_out = conv_b_out[j:j + 1]
        if i + 1 < depth:
            shift_n, scale_n, gate_n = mod_rows(i + 1)
            xs, h = _outproj(y, w_out.astype(BF16), xs, gate, bias_out, norm_g[i + 1:i + 2], scale_n, shift_n,
                             modulate=True, write_x=True, out_dtype=BF16, pre=pre)
            gate = gate_n
        else:
            zero = jnp.zeros((1, d), F32)
            out, = _outproj(y, w_out.astype(BF16), xs, gate, bias_out, final_g.reshape(1, d), zero, zero,
                            modulate=False, write_x=False, out_dtype=F32, pre=pre)
    return out.reshape(bsz, s, d)
```

```python
import functools
import math

import numpy as np
import jax
import jax.numpy as jnp
from jax import lax
from jax.experimental import pallas as pl
from jax.experimental.pallas import tpu as pltpu

F32 = jnp.float32
BF16 = jnp.bfloat16

EPS = 1e-6
LOG2E = 1.4426950408889634
N_MIXERS = 3

DA_HEADS = 8
DA_QK_DIM = 128
DA_V_DIM = 256
ATT_SKIP_LOG2 = 160.0
ATT_NORM_MARGIN = 1.01
HG_HEADS = 16
HG_DIM = 128
HG_CHUNK = 64
HG_LEVELS = (32, 16, 8)
HG_SAFE_EXP = 80.0
CONV_WIDTH = 31
CONV_HALO = 32

VMEM_LIMIT_BYTES = 56 * 1024 * 1024


def _cparams(*sem):
    return pltpu.CompilerParams(dimension_semantics=sem, vmem_limit_bytes=VMEM_LIMIT_BYTES)


def _sigmoid(x):
    return 0.5 * jnp.tanh(0.5 * x) + 0.5


def _silu(x):
    h = 0.5 * x
    return h + h * jnp.tanh(h)


def _mods_kernel(c_ref, w_ref, b_ref, o_ref, ca_ref, acc_ref, *, kc, ln):
    kstep = pl.program_id(1)
    kb, n = w_ref.shape[1], w_ref.shape[2]
    ca_ref[...] = _silu(c_ref[...])

    @pl.when(kstep == 0)
    def _():
        acc_ref[...] = jnp.zeros_like(acc_ref)

    for lb in range(n // ln):
        cols = slice(lb * ln, (lb + 1) * ln)

        def body(i, acc):
            k0 = pl.multiple_of(i * kc, kc)
            prod = w_ref[0, pl.ds(k0, kc), cols] * ca_ref[pl.ds(k0, kc), :]
            return acc + prod.reshape(kc // 8, 8, ln).sum(axis=0)

        acc_ref[:, cols] = lax.fori_loop(0, kb // kc, body, acc_ref[:, cols], unroll=4)

    @pl.when(kstep == pl.num_programs(1) - 1)
    def _():
        o_ref[0] = jnp.sum(acc_ref[...], axis=0, keepdims=True) + b_ref[0]


def _mods(c, ada_w, ada_b, *, kb=512, kc=32, ln=1024):
    depth, d, n = ada_w.shape
    return pl.pallas_call(
        functools.partial(_mods_kernel, kc=kc, ln=ln),
        grid=(depth, d // kb),
        in_specs=[
            pl.BlockSpec((kb, 1), lambda i, k: (k, 0)),
            pl.BlockSpec((1, kb, n), lambda i, k: (i, k, 0)),
            pl.BlockSpec((1, 1, n), lambda i, k: (i, 0, 0)),
        ],
        out_specs=pl.BlockSpec((1, 1, n), lambda i, k: (i, 0, 0)),
        out_shape=jax.ShapeDtypeStruct((depth, 1, n), F32),
        scratch_shapes=[pltpu.VMEM((kb, 1), F32), pltpu.VMEM((8, n), F32)],
        compiler_params=_cparams("arbitrary", "arbitrary"),
        name="adaln_mods",
    )(c.reshape(d, 1), ada_w, ada_b.reshape(depth, 1, n))


def _norm_kernel(x_ref, g_ref, sc_ref, sh_ref, o_ref, *, modulate):
    x = x_ref[...]
    y = x * lax.rsqrt(jnp.mean(x * x, axis=-1, keepdims=True) + EPS) * g_ref[...]
    if modulate:
        y = y * (1.0 + sc_ref[...]) + sh_ref[...]
    o_ref[...] = y.astype(o_ref.dtype)


def _norm(x, gain, scale, shift, *, modulate, out_dtype, tm=512):
    s, d = x.shape
    row = pl.BlockSpec((1, d), lambda i: (0, 0))
    return pl.pallas_call(
        functools.partial(_norm_kernel, modulate=modulate),
        grid=(s // tm,),
        in_specs=[pl.BlockSpec((tm, d), lambda i: (i, 0)), row, row, row],
        out_specs=pl.BlockSpec((tm, d), lambda i: (i, 0)),
        out_shape=jax.ShapeDtypeStruct((s, d), out_dtype),
        compiler_params=_cparams("arbitrary"),
        name="rmsnorm_mod",
    )(x, gain, scale, shift)


def _mm_kernel(*refs, has_bias, has_colscale):
    a_ref, w_ref = refs[0], refs[1]
    o_ref, wb_ref = refs[-2], refs[-1]
    pos = 2

    @pl.when(pl.program_id(1) == 0)
    def _():
        wb_ref[...] = w_ref[...].astype(BF16)

    acc = jnp.dot(a_ref[...], wb_ref[...], preferred_element_type=F32)
    if has_bias:
        acc = acc + refs[pos][...]
        pos += 1
    if has_colscale:
        acc = acc * refs[pos][...]
    o_ref[...] = acc.astype(o_ref.dtype)


def _matmul(a, w, layer, *, out_dtype, bias=None, colscale=None, tm=1024, tn=1024):
    m, k = a.shape
    n_out = w.shape[2]
    in_specs = [
        pl.BlockSpec((tm, k), lambda j, i: (i, 0)),
        pl.BlockSpec((None, k, tn), lambda j, i: (layer, 0, j)),
    ]
    args = [a, w]
    rowspec = pl.BlockSpec((1, tn), lambda j, i: (0, j))
    for extra in (bias, colscale):
        if extra is not None:
            in_specs.append(rowspec)
            args.append(extra)
    return pl.pallas_call(
        functools.partial(_mm_kernel, has_bias=bias is not None, has_colscale=colscale is not None),
        grid=(n_out // tn, m // tm),
        in_specs=in_specs,
        out_specs=pl.BlockSpec((tm, tn), lambda j, i: (i, j)),
        out_shape=jax.ShapeDtypeStruct((m, n_out), out_dtype),
        scratch_shapes=[pltpu.VMEM((k, tn), BF16)],
        compiler_params=_cparams("arbitrary", "arbitrary"),
        name="matmul_in",
    )(*args)


def _outproj_kernel(*refs, has_bias, has_pre, modulate, write_x):
    a_ref, w_ref, x_ref, gate_ref = refs[:4]
    pos = 4
    a = a_ref[...]
    if has_pre:
        a = a * lax.rsqrt(jnp.mean(a * a, axis=-1, keepdims=True) + EPS) * refs[pos + 1][...]
        a = (a * _silu(refs[pos][...])).astype(BF16)
        pos += 2
    y = jnp.dot(a, w_ref[...], preferred_element_type=F32)
    if has_bias:
        y = y + refs[pos][...]
        pos += 1
    g_ref, sc_ref, sh_ref = refs[pos:pos + 3]
    xn = x_ref[...] + gate_ref[...] * y
    if write_x:
        refs[-2][...] = xn
    hn = xn * lax.rsqrt(jnp.mean(xn * xn, axis=-1, keepdims=True) + EPS) * g_ref[...]
    if modulate:
        hn = hn * (1.0 + sc_ref[...]) + sh_ref[...]
    refs[-1][...] = hn.astype(refs[-1].dtype)


def _outproj(a, w, x, gate, bias, gain, scale, shift, *, modulate, write_x, out_dtype, pre=None, tm=None):
    m, k = a.shape
    d = w.shape[1]
    if tm is None:
        tm = 256 if pre is not None else 512
    row = pl.BlockSpec((1, d), lambda i: (0, 0))
    tile = pl.BlockSpec((tm, d), lambda i: (i, 0))
    in_specs = [pl.BlockSpec((tm, k), lambda i: (i, 0)), pl.BlockSpec((k, d), lambda i: (0, 0)), tile, row]
    args = [a, w, x, gate]
    if pre is not None:
        u, col_block, pre_gain = pre
        in_specs += [pl.BlockSpec((tm, k), lambda i: (i, col_block)), pl.BlockSpec((1, k), lambda i: (0, 0))]
        args += [u, pre_gain]
    if bias is not None:
        in_specs.append(row)
        args.append(bias)
    in_specs += [row, row, row]
    args += [gain, scale, shift]
    out_specs = [tile]
    out_shape = [jax.ShapeDtypeStruct((m, d), out_dtype)]
    if write_x:
        out_specs = [tile, tile]
        out_shape = [jax.ShapeDtypeStruct((m, d), F32)] + out_shape
    return pl.pallas_call(
        functools.partial(_outproj_kernel, has_bias=bias is not None, has_pre=pre is not None,
                          modulate=modulate, write_x=write_x),
        grid=(m // tm,),
        in_specs=in_specs,
        out_specs=out_specs,
        out_shape=out_shape,
        compiler_params=_cparams("arbitrary"),
        name="matmul_out",
    )(*args)


def _attn_kernel(slope_ref, lam_ref, qn_ref, kn_ref, q_ref, k_ref, v_ref, g_ref, sg_ref, o_ref,
                 acc_ref, l_ref, s_ref, *, tq, out_scale):
    h = pl.program_id(0)
    i = pl.program_id(1)
    dk = DA_QK_DIM
    slope = slope_ref[h]
    lam = lam_ref[0]
    lanes = 128
    q0 = i * tq

    def scores(mi, k0, tk, m_old, diag_col):
        colf = lax.broadcasted_iota(jnp.int32, (1, tk), 1).astype(F32)
        bias = slope * (colf + (k0 - q0).astype(F32))
        qm = q_ref[:, mi * dk:(mi + 1) * dk]
        km = k_ref[pl.ds(k0, tk), mi * dk:(mi + 1) * dk]
        s = lax.dot_general(qm, km, (((1,), (1,)), ((), ())), preferred_element_type=F32) + bias
        if diag_col is not None:
            r = lax.broadcasted_iota(jnp.int32, (tq, tk), 0)
            c = lax.broadcasted_iota(jnp.int32, (tq, tk), 1)
            s = jnp.where(r + diag_col >= c, s, -jnp.inf)
        s_ref[mi, :, 0:tk] = s
        smax = jnp.max(s, axis=-1, keepdims=True)
        return smax if m_old is None else jnp.maximum(m_old, smax)

    def accumulate(mi, k0, tk, m_old, m_new):
        vb = v_ref[pl.ds(k0, tk), :]
        p = jnp.exp2(s_ref[mi, :, 0:tk] - m_new)
        psum = p[:, 0:lanes]
        for t in range(1, tk // lanes):
            psum = psum + p[:, t * lanes:(t + 1) * lanes]
        if m_old is None:
            l_ref[mi] = psum
            acc_ref[mi] = jnp.dot(p.astype(BF16), vb, preferred_element_type=F32)
        else:
            alpha = jnp.exp2(m_old - m_new)
            l_ref[mi] = alpha * l_ref[mi] + psum
            acc_ref[mi] = alpha * acc_ref[mi] + jnp.dot(p.astype(BF16), vb, preferred_element_type=F32)

    def tile(k0, tk, m, diag_col=None):
        k0 = pl.multiple_of(k0, tq)
        old = (None, None) if m is None else m
        new = tuple(scores(mi, k0, tk, old[mi], diag_col) for mi in range(2))
        for mi in range(2):
            accumulate(mi, k0, tk, old[mi], new[mi])
        return new

    odd = i % 2 == 1
    m = lax.cond(odd, lambda: tile(q0 - tq, 2 * tq, None, diag_col=tq), lambda: tile(q0, tq, None, diag_col=0))

    qn = [qn_ref[2 * h + mi, i] * ATT_NORM_MARGIN for mi in range(2)]
    floor = [jnp.min(m[mi]) - ATT_SKIP_LOG2 for mi in range(2)]

    def dead(k0, tk):
        b0 = k0 // tq
        top_bias = slope * (k0 + (tk - 1) - q0).astype(F32)
        ok = None
        for mi in range(2):
            kmax = kn_ref[2 * h + mi, b0]
            for t in range(1, tk // tq):
                kmax = jnp.maximum(kmax, kn_ref[2 * h + mi, b0 + t])
            below = qn[mi] * kmax + top_bias < floor[mi]
            ok = below if ok is None else jnp.logical_and(ok, below)
        return ok

    def visit(k0, w, m, take=True):
        half = w // 2
        near_only = functools.partial(lambda m, k0, half: tile(k0 + half, half, m), k0=k0, half=half)
        whole = functools.partial(lambda m, k0, w: tile(k0, w, m), k0=k0, w=w)
        live = jnp.logical_and(take, jnp.logical_not(dead(k0, w)))
        return lax.cond(live, lambda m: lax.cond(dead(k0, half), near_only, whole, m), lambda m: m, m)

    wide = s_ref.shape[2]
    w = 2 * tq
    end = q0 - jnp.where(odd, tq, 0)
    while w < wide:
        take = (i // (w // tq)) % 2 == 1
        m = visit(jnp.where(take, end - w, 0), w, m, take)
        end = end - jnp.where(take, w, 0)
        w *= 2
    lax.fori_loop(0, i // (wide // tq), lambda jj, m: visit(jj * wide, wide, m), m)

    inv1 = 1.0 / jnp.sum(l_ref[0], axis=-1, keepdims=True)
    inv2 = lam / jnp.sum(l_ref[1], axis=-1, keepdims=True)
    o = acc_ref[0] * inv1 - acc_ref[1] * inv2
    y = o * lax.rsqrt(jnp.mean(o * o, axis=-1, keepdims=True) + EPS) * sg_ref[...] * out_scale
    o_ref[...] = (y * _silu(g_ref[...].astype(F32))).astype(o_ref.dtype)


def _block_norm_kernel(x_ref, g_ref, o_ref):
    x = x_ref[...]
    rs = jnp.dot(x * x, g_ref[...], preferred_element_type=F32)
    o_ref[0] = jnp.sqrt(jnp.max(rs, axis=0, keepdims=True))


def _qk_block_norms(u, *, tq):
    s = u.shape[0]
    groups = 4 * DA_HEADS
    cols = groups * DA_QK_DIM
    ind = np.zeros((cols, 128), np.float32)
    ind[np.arange(cols), np.arange(cols) // DA_QK_DIM] = 1.0
    out = pl.pallas_call(
        _block_norm_kernel,
        grid=(s // tq,),
        in_specs=[pl.BlockSpec((tq, cols), lambda i: (i, 0)), pl.BlockSpec((cols, 128), lambda i: (0, 0))],
        out_specs=pl.BlockSpec((1, 1, 128), lambda i: (i, 0, 0)),
        out_shape=jax.ShapeDtypeStruct((s // tq, 1, 128), F32),
        compiler_params=_cparams("arbitrary"),
        name="qk_block_norms",
    )(u, jnp.asarray(ind, BF16))
    return out[:, 0, :groups].T


def _attention(u, slopes, lam, subln_g, *, out_scale, tq=512, tk_wide=2048):
    s = u.shape[0]
    dv = DA_V_DIM
    nh = DA_HEADS
    smem = pl.BlockSpec(memory_space=pltpu.SMEM)
    norms = _qk_block_norms(u, tq=tq)
    return pl.pallas_call(
        functools.partial(_attn_kernel, tq=tq, out_scale=out_scale),
        grid=(nh, s // tq),
        in_specs=[
            smem, smem, smem, smem,
            pl.BlockSpec((tq, dv), lambda h, i: (i, h)),
            pl.BlockSpec((s, dv), lambda h, i: (0, nh + h)),
            pl.BlockSpec((s, dv), lambda h, i: (0, 2 * nh + h)),
            pl.BlockSpec((tq, dv), lambda h, i: (i, 3 * nh + h)),
            pl.BlockSpec((1, dv), lambda h, i: (0, 0)),
        ],
        out_specs=pl.BlockSpec((tq, dv), lambda h, i: (i, h)),
        out_shape=jax.ShapeDtypeStruct((s, nh * dv), BF16),
        scratch_shapes=[pltpu.VMEM((2, tq, dv), F32), pltpu.VMEM((2, tq, 128), F32),
                        pltpu.VMEM((2, tq, tk_wide), F32)],
        compiler_params=_cparams("arbitrary", "arbitrary"),
        name="diff_attention",
    )(slopes, lam, norms[:2 * nh], norms[2 * nh:], u, u, u, u, subln_g)


def _hgrn_exponent_matrix():
    c = HG_CHUNK
    w = np.zeros((len(HG_LEVELS) + 1, c, c), np.float32)
    for r in range(c):
        w[0, r, :r + 1] = 1.0
        for li, m in enumerate(HG_LEVELS):
            mid = (r // (2 * m)) * 2 * m + m
            if r % (2 * m) >= m:
                w[li + 1, r, mid:r + 1] = 1.0
            else:
                w[li + 1, r, r + 1:mid] = 1.0
    return w.reshape(-1, c)


def _hgrn_kernel(q_ref, f_ref, i_ref, lb_ref, w_ref, o_ref, st_ref, st0_ref, *, tt):
    c = HG_CHUNK

    @pl.when(pl.program_id(1) == 0)
    def _():
        st_ref[...] = jnp.zeros_like(st_ref)

    lb = lb_ref[...]
    wall = w_ref[...]
    row = lax.broadcasted_iota(jnp.int32, (c, 1), 0)
    r2 = lax.broadcasted_iota(jnp.int32, (c, c), 0)
    c2 = lax.broadcasted_iota(jnp.int32, (c, c), 1)
    row8 = lax.broadcasted_iota(jnp.int32, (8, 1), 0)
    nt = (((1,), (1,)), ((), ()))
    tn = (((0,), (0,)), ((), ()))

    def chunk(ci, carry):
        r0 = pl.multiple_of(ci * c, c)
        q = _silu(q_ref[pl.ds(r0, c), :])
        v = i_ref[pl.ds(r0, c), :]
        fg = lb + (1.0 - lb) * jax.nn.sigmoid(f_ref[pl.ds(r0, c), :])
        kk = 1.0 - fg
        g = jnp.log(fg)
        g_hi = g.astype(BF16)
        g_lo = (g - g_hi.astype(F32)).astype(BF16)
        ex = (jnp.dot(wall, g_hi, preferred_element_type=F32)
              + jnp.dot(wall, g_lo, preferred_element_type=F32))
        b = ex[0:c]
        st = st_ref[...]
        vb = v.astype(BF16)

        o = lax.dot_general((q * jnp.exp(b)).astype(BF16), st.astype(BF16), nt,
                            preferred_element_type=F32)

        a = jnp.zeros((c, c), F32)
        for li, m in enumerate(HG_LEVELS):
            e = jnp.exp(ex[(li + 1) * c:(li + 2) * c])
            upper = (row % (2 * m)) >= m
            qt = jnp.where(upper, q * e, 0.0).astype(BF16)
            kt = jnp.where(upper, 0.0, kk * e).astype(BF16)
            al = lax.dot_general(qt, kt, nt, preferred_element_type=F32)
            if 2 * m != c:
                al = jnp.where((r2 // (2 * m)) == (c2 // (2 * m)), al, 0.0)
            a = a + al
        o = o + jnp.dot(a.astype(BF16), vb, preferred_element_type=F32)

        parts = []
        for blk in range(c // 8):
            sl = slice(8 * blk, 8 * blk + 8)
            bb, kb, v8, q8 = b[sl], kk[sl], v[sl], q[sl]
            od = jnp.zeros((8, HG_DIM), F32)
            for s in range(8):
                e = jnp.exp(jnp.where(row8 >= s, bb - bb[s:s + 1], -jnp.inf))
                w = jnp.sum(e * q8 * kb[s:s + 1], axis=-1, keepdims=True)
                od = od + w * v8[s:s + 1]
            parts.append(od)
        o_ref[pl.ds(r0, c), :] = o + jnp.concatenate(parts, axis=0)

        b_last = b[c - 1:c]
        kd = (kk * jnp.exp(b_last - b)).astype(BF16)
        st_ref[...] = jnp.exp(b_last) * st + lax.dot_general(vb, kd, tn, preferred_element_type=F32)
        return carry

    n_chunks = tt // c
    dk = HG_DIM
    st0_ref[...] = st_ref[...]

    def side_by_side(fn):
        return jnp.concatenate([fn(n * c) for n in range(n_chunks)], axis=1)

    fg = side_by_side(lambda r0: lb + (1.0 - lb) * jax.nn.sigmoid(f_ref[r0:r0 + c, :]))
    g = jnp.log(fg)
    g_hi = g.astype(BF16)
    g_lo = (g - g_hi.astype(F32)).astype(BF16)
    tril = wall[0:c]
    b = (jnp.dot(tril, g_hi, preferred_element_type=F32)
         + jnp.dot(tril, g_lo, preferred_element_type=F32))
    safe = jnp.min(b[c - 1:c, :]) >= -HG_SAFE_EXP
    eb = jnp.exp(b)
    eb_last = eb[c - 1:c, :]
    qb = (side_by_side(lambda r0: _silu(q_ref[r0:r0 + c, :])) * eb).astype(BF16)
    ke = (1.0 - fg) * jnp.exp(-b)
    keb = ke.astype(BF16)
    kd = (ke * eb_last).astype(BF16)
    vbs = [i_ref[n * c:(n + 1) * c, :].astype(BF16) for n in range(n_chunks)]
    lanes_of = lambda x, n: x[:, n * dk:(n + 1) * dk]
    causal = r2 >= c2
    a = [jnp.where(causal, lax.dot_general(lanes_of(qb, n), lanes_of(keb, n), nt, preferred_element_type=F32),
                   0.0).astype(BF16) for n in range(n_chunks)]
    o_intra = [jnp.dot(a[n], vbs[n], preferred_element_type=F32) for n in range(n_chunks)]
    upd = [lax.dot_general(vbs[n], lanes_of(kd, n), tn, preferred_element_type=F32) for n in range(n_chunks)]
    st = st_ref[...]
    for n in range(n_chunks):
        o_ref[n * c:(n + 1) * c, :] = o_intra[n] + lax.dot_general(
            lanes_of(qb, n), st.astype(BF16), nt, preferred_element_type=F32)
        st = lanes_of(eb_last, n) * st + upd[n]
    st_ref[...] = st

    @pl.when(jnp.logical_not(safe))
    def _():
        st_ref[...] = st0_ref[...]
        lax.fori_loop(0, n_chunks, chunk, 0, unroll=2)


def _hgrn_core(u, lb, *, tt=2048):
    s = u.shape[0]
    nh, dk = HG_HEADS, HG_DIM
    wall = jnp.asarray(_hgrn_exponent_matrix(), BF16)
    return pl.pallas_call(
        functools.partial(_hgrn_kernel, tt=tt),
        grid=(nh, s // tt),
        in_specs=[
            pl.BlockSpec((tt, dk), lambda h, t: (t, h)),
            pl.BlockSpec((tt, dk), lambda h, t: (t, nh + h)),
            pl.BlockSpec((tt, dk), lambda h, t: (t, 2 * nh + h)),
            pl.BlockSpec((1, dk), lambda h, t: (0, h)),
            pl.BlockSpec(wall.shape, lambda h, t: (0, 0)),
        ],
        out_specs=pl.BlockSpec((tt, dk), lambda h, t: (t, h)),
        out_shape=jax.ShapeDtypeStruct((s, nh * dk), F32),
        scratch_shapes=[pltpu.VMEM((dk, dk), F32), pltpu.VMEM((dk, dk), F32)],
        compiler_params=_cparams("arbitrary", "arbitrary"),
        name="hgrn2_core",
    )(u, u, u, lb, wall)


def _conv_kernel(a_ref, ag_ref, gate_ref, dw_ref, dwb_ref, lng_ref, lnb_ref, o_ref, ybuf, cbuf, shbuf,
                 *, tt, cw):
    halo = CONV_HALO
    d = a_ref.shape[1]

    @pl.when(pl.program_id(0) == 0)
    def _():
        ybuf[0:halo, :] = jnp.zeros((halo, d), F32)

    @pl.when(pl.program_id(0) > 0)
    def _():
        ybuf[0:halo, :] = ybuf[tt:tt + halo, :]

    strip = 16

    def glu(si, carry):
        r = pl.multiple_of(si * strip, strip)
        ybuf[pl.ds(pl.multiple_of(halo + r, strip), strip), :] = (
            a_ref[pl.ds(r, strip), :] * _sigmoid(ag_ref[pl.ds(r, strip), :]))
        return carry

    lax.fori_loop(0, tt // strip, glu, 0)
    first = halo - (CONV_WIDTH - 1)

    def lanes(ci, carry):
        c0 = pl.multiple_of(ci * cw, cw)
        acc = jnp.broadcast_to(dwb_ref[:, pl.ds(c0, cw)], (tt, cw))
        for rho in range(8):
            taps = [k for k in range(CONV_WIDTH) if (first + k) % 8 == rho]
            if not taps:
                continue
            span = tt + 8 * max((first + k) // 8 for k in taps)
            if rho:
                shbuf[rho, 0:span, :] = ybuf[rho:rho + span, pl.ds(c0, cw)]
            for k in taps:
                a8 = 8 * ((first + k) // 8)
                yk = shbuf[rho, a8:a8 + tt, :] if rho else ybuf[a8:a8 + tt, pl.ds(c0, cw)]
                acc = acc + yk * dw_ref[k:k + 1, pl.ds(c0, cw)]
        cbuf[:, pl.ds(c0, cw)] = acc
        return carry

    lax.fori_loop(0, d // cw, lanes, 0)

    def layernorm_gate(si, carry):
        r = pl.multiple_of(si * strip, strip)
        y = cbuf[pl.ds(r, strip), :]
        mu = jnp.mean(y, axis=-1, keepdims=True)
        yc = y - mu
        var = jnp.mean(yc * yc, axis=-1, keepdims=True)
        z = yc * lax.rsqrt(var + EPS) * lng_ref[...] + lnb_ref[...]
        o_ref[pl.ds(r, strip), :] = (_silu(z) * _silu(gate_ref[pl.ds(r, strip), :])).astype(o_ref.dtype)
        return carry

    lax.fori_loop(0, tt // strip, layernorm_gate, 0, unroll=4)


def _conv_core(u, dw, dw_b, ln_g, ln_b, *, tt=256, cw=128):
    s = u.shape[0]
    d = dw.shape[1]
    row = pl.BlockSpec((1, d), lambda t: (0, 0))
    return pl.pallas_call(
        functools.partial(_conv_kernel, tt=tt, cw=cw),
        grid=(s // tt,),
        in_specs=[
            pl.BlockSpec((tt, d), lambda t: (t, 0)),
            pl.BlockSpec((tt, d), lambda t: (t, 1)),
            pl.BlockSpec((tt, d), lambda t: (t, 2)),
            pl.BlockSpec(dw.shape, lambda t: (0, 0)),
            row, row, row,
        ],
        out_specs=pl.BlockSpec((tt, d), lambda t: (t, 0)),
        out_shape=jax.ShapeDtypeStruct((s, d), BF16),
        scratch_shapes=[pltpu.VMEM((tt + CONV_HALO, d), F32), pltpu.VMEM((tt, d), F32),
                        pltpu.VMEM((8, tt + CONV_HALO, cw), F32)],
        compiler_params=_cparams("arbitrary"),
        name="conv_module_core",
    )(u, u, u, dw, dw_b, ln_g, ln_b)


def kernel(x, c, norm_g, ada_w, ada_b, attn_w_in, attn_w_out, attn_lam_q1, attn_lam_k1, attn_lam_q2, attn_lam_k2, attn_subln_g, hgrn_w_in, hgrn_w_out, hgrn_lb_logits, hgrn_gnorm_g, conv_w_in, conv_b_in, conv_dw, conv_dw_b, conv_ln_g, conv_ln_b, conv_w_out, conv_b_out, final_g):
    bsz, s, d = x.shape
    depth = norm_g.shape[0]
    assert bsz == 1 and c.shape == (1, d)
    xs = x.reshape(s, d)
    mods = _mods(c, ada_w, ada_b)

    slopes = (2.0 ** (-8.0 * jnp.arange(1, DA_HEADS + 1, dtype=F32) / DA_HEADS)) * LOG2E
    qk_cols = 2 * DA_HEADS * DA_QK_DIM
    attn_colscale = jnp.concatenate(
        [jnp.full((1, qk_cols), DA_QK_DIM ** -0.5 * LOG2E, F32), jnp.ones((1, 3 * qk_cols), F32)], axis=1)
    lb_all = jax.nn.softmax(hgrn_lb_logits.astype(F32), axis=0)
    lb_all = jnp.cumsum(lb_all, axis=0) - lb_all[0]

    def mod_rows(i):
        return tuple(mods[i, :, k * d:(k + 1) * d] for k in range(3))

    shift, scale, gate = mod_rows(0)
    h = _norm(xs, norm_g[0:1], scale, shift, modulate=True, out_dtype=BF16)
    for i in range(depth):
        kind, j = i % N_MIXERS, i // N_MIXERS
        bias_out = pre = None
        if kind == 0:
            u = _matmul(h, attn_w_in, j, out_dtype=BF16, colscale=attn_colscale)
            lam_init = 0.8 - 0.6 * math.exp(-0.3 * i)
            lam = (jnp.exp(jnp.sum(attn_lam_q1[j] * attn_lam_k1[j]))
                   - jnp.exp(jnp.sum(attn_lam_q2[j] * attn_lam_k2[j])) + lam_init).reshape(1)
            y = _attention(u, slopes, lam, attn_subln_g[j:j + 1], out_scale=1.0 - lam_init)
            w_out = attn_w_out[j]
        elif kind == 1:
            u = _matmul(h, hgrn_w_in, j, out_dtype=F32)
            y = _hgrn_core(u, lb_all[i:i + 1])
            pre = (u, u.shape[1] // d - 1, hgrn_gnorm_g[j:j + 1])
            w_out = hgrn_w_out[j]
        else:
            u = _matmul(h, conv_w_in, j, out_dtype=F32, bias=conv_b_in[j:j + 1])
            y = _conv_core(u, conv_dw[j], conv_dw_b[j:j + 1], conv_ln_g[j:j + 1], conv_ln_b[j:j + 1])
            w_out = conv_w_out[j]
            bias_out = conv_b_out[j:j + 1]
        if i + 1 < depth:
            shift_n, scale_n, gate_n = mod_rows(i + 1)
            xs, h = _outproj(y, w_out.astype(BF16), xs, gate, bias_out, norm_g[i + 1:i + 2], scale_n, shift_n,
                             modulate=True, write_x=True, out_dtype=BF16, pre=pre)
            gate = gate_n
        else:
            zero = jnp.zeros((1, d), F32)
            out, = _outproj(y, w_out.astype(BF16), xs, gate, bias_out, final_g.reshape(1, d), zero, zero,
                            modulate=False, write_x=False, out_dtype=F32, pre=pre)
    return out.reshape(bsz, s, d)
```

```python
import functools
import math

import numpy as np
import jax
import jax.numpy as jnp
from jax import lax
from jax.experimental import pallas as pl
from jax.experimental.pallas import tpu as pltpu

F32 = jnp.float32
BF16 = jnp.bfloat16

EPS = 1e-6
LOG2E = 1.4426950408889634
N_MIXERS = 3

DA_HEADS = 8
DA_QK_DIM = 128
DA_V_DIM = 256
ATT_SKIP_LOG2 = 160.0
ATT_NORM_MARGIN = 1.01
HG_HEADS = 16
HG_DIM = 128
HG_CHUNK = 64
HG_LEVELS = (32, 16, 8)
HG_SAFE_EXP = 80.0
CONV_WIDTH = 31
CONV_HALO = 32

SUBLANES = 8
LANES = 128
VMEM_LIMIT_BYTES = 56 * 1024 * 1024


def _cparams(*sem):
    return pltpu.CompilerParams(dimension_semantics=sem, vmem_limit_bytes=VMEM_LIMIT_BYTES)


def _sigmoid(x):
    return 0.5 * jnp.tanh(0.5 * x) + 0.5


def _silu(x):
    h = 0.5 * x
    return h + h * jnp.tanh(h)


def _mods_kernel(c_ref, w_ref, b_ref, o_ref, ca_ref, acc_ref, *, kc, ln):
    kstep = pl.program_id(1)
    kb, n = w_ref.shape[1], w_ref.shape[2]
    ca_ref[...] = _silu(c_ref[...])

    @pl.when(kstep == 0)
    def _():
        acc_ref[...] = jnp.zeros_like(acc_ref)

    for lb in range(n // ln):
        cols = slice(lb * ln, (lb + 1) * ln)

        def body(i, acc):
            k0 = pl.multiple_of(i * kc, kc)
            prod = w_ref[0, pl.ds(k0, kc), cols] * ca_ref[pl.ds(k0, kc), :]
            return acc + prod.reshape(kc // SUBLANES, SUBLANES, ln).sum(axis=0)

        acc_ref[:, cols] = lax.fori_loop(0, kb // kc, body, acc_ref[:, cols], unroll=4)

    @pl.when(kstep == pl.num_programs(1) - 1)
    def _():
        o_ref[0] = jnp.sum(acc_ref[...], axis=0, keepdims=True) + b_ref[0]


def _mods(c, ada_w, ada_b, *, kb=512, kc=32, ln=1024):
    depth, d, n = ada_w.shape
    return pl.pallas_call(
        functools.partial(_mods_kernel, kc=kc, ln=ln),
        grid=(depth, d // kb),
        in_specs=[
            pl.BlockSpec((kb, 1), lambda i, k: (k, 0)),
            pl.BlockSpec((1, kb, n), lambda i, k: (i, k, 0)),
            pl.BlockSpec((1, 1, n), lambda i, k: (i, 0, 0)),
        ],
        out_specs=pl.BlockSpec((1, 1, n), lambda i, k: (i, 0, 0)),
        out_shape=jax.ShapeDtypeStruct((depth, 1, n), F32),
        scratch_shapes=[pltpu.VMEM((kb, 1), F32), pltpu.VMEM((SUBLANES, n), F32)],
        compiler_params=_cparams("arbitrary", "arbitrary"),
        name="adaln_mods",
    )(c.reshape(d, 1), ada_w, ada_b.reshape(depth, 1, n))


def _norm_kernel(x_ref, g_ref, sc_ref, sh_ref, o_ref):
    x = x_ref[...]
    y = x * lax.rsqrt(jnp.mean(x * x, axis=-1, keepdims=True) + EPS) * g_ref[...]
    o_ref[...] = (y * (1.0 + sc_ref[...]) + sh_ref[...]).astype(o_ref.dtype)


def _norm(x, gain, scale, shift, *, tm=512):
    s, d = x.shape
    row = pl.BlockSpec((1, d), lambda i: (0, 0))
    return pl.pallas_call(
        _norm_kernel,
        grid=(s // tm,),
        in_specs=[pl.BlockSpec((tm, d), lambda i: (i, 0)), row, row, row],
        out_specs=pl.BlockSpec((tm, d), lambda i: (i, 0)),
        out_shape=jax.ShapeDtypeStruct((s, d), BF16),
        compiler_params=_cparams("arbitrary"),
        name="rmsnorm_mod",
    )(x, gain, scale, shift)


def _mm_kernel(*refs, has_bias, has_colscale):
    a_ref, w_ref = refs[0], refs[1]
    o_ref, wb_ref = refs[-2], refs[-1]
    pos = 2

    @pl.when(pl.program_id(1) == 0)
    def _():
        wb_ref[...] = w_ref[...].astype(BF16)

    acc = jnp.dot(a_ref[...], wb_ref[...], preferred_element_type=F32)
    if has_bias:
        acc = acc + refs[pos][...]
        pos += 1
    if has_colscale:
        acc = acc * refs[pos][...]
    o_ref[...] = acc.astype(o_ref.dtype)


def _matmul(a, w, layer, *, out_dtype, bias=None, colscale=None, tm=1024, tn=1024):
    m, k = a.shape
    n_out = w.shape[2]
    in_specs = [
        pl.BlockSpec((tm, k), lambda j, i: (i, 0)),
        pl.BlockSpec((None, k, tn), lambda j, i: (layer, 0, j)),
    ]
    args = [a, w]
    rowspec = pl.BlockSpec((1, tn), lambda j, i: (0, j))
    for extra in (bias, colscale):
        if extra is not None:
            in_specs.append(rowspec)
            args.append(extra)
    return pl.pallas_call(
        functools.partial(_mm_kernel, has_bias=bias is not None, has_colscale=colscale is not None),
        grid=(n_out // tn, m // tm),
        in_specs=in_specs,
        out_specs=pl.BlockSpec((tm, tn), lambda j, i: (i, j)),
        out_shape=jax.ShapeDtypeStruct((m, n_out), out_dtype),
        scratch_shapes=[pltpu.VMEM((k, tn), BF16)],
        compiler_params=_cparams("arbitrary", "arbitrary"),
        name="matmul_in",
    )(*args)


def _outproj_kernel(*refs, has_bias, has_pre, modulate, write_x):
    a_ref, w_ref, x_ref, gate_ref = refs[:4]
    pos = 4
    a = a_ref[...]
    if has_pre:
        a = a * lax.rsqrt(jnp.mean(a * a, axis=-1, keepdims=True) + EPS) * refs[pos + 1][...]
        a = (a * _silu(refs[pos][...])).astype(BF16)
        pos += 2
    y = jnp.dot(a, w_ref[...], preferred_element_type=F32)
    if has_bias:
        y = y + refs[pos][...]
        pos += 1
    g_ref, sc_ref, sh_ref = refs[pos:pos + 3]
    xn = x_ref[...] + gate_ref[...] * y
    if write_x:
        refs[-2][...] = xn
    hn = xn * lax.rsqrt(jnp.mean(xn * xn, axis=-1, keepdims=True) + EPS) * g_ref[...]
    if modulate:
        hn = hn * (1.0 + sc_ref[...]) + sh_ref[...]
    refs[-1][...] = hn.astype(refs[-1].dtype)


def _outproj(a, w, x, gate, bias, gain, scale, shift, *, modulate, write_x, out_dtype, pre=None, tm=None):
    m, k = a.shape
    d = w.shape[1]
    if tm is None:
        tm = 256 if pre is not None else 512
    row = pl.BlockSpec((1, d), lambda i: (0, 0))
    tile = pl.BlockSpec((tm, d), lambda i: (i, 0))
    in_specs = [pl.BlockSpec((tm, k), lambda i: (i, 0)), pl.BlockSpec((k, d), lambda i: (0, 0)), tile, row]
    args = [a, w, x, gate]
    if pre is not None:
        u, col_block, pre_gain = pre
        in_specs += [pl.BlockSpec((tm, k), lambda i: (i, col_block)), pl.BlockSpec((1, k), lambda i: (0, 0))]
        args += [u, pre_gain]
    if bias is not None:
        in_specs.append(row)
        args.append(bias)
    in_specs += [row, row, row]
    args += [gain, scale, shift]
    out_specs = [tile]
    out_shape = [jax.ShapeDtypeStruct((m, d), out_dtype)]
    if write_x:
        out_specs = [tile, tile]
        out_shape = [jax.ShapeDtypeStruct((m, d), F32)] + out_shape
    return pl.pallas_call(
        functools.partial(_outproj_kernel, has_bias=bias is not None, has_pre=pre is not None,
                          modulate=modulate, write_x=write_x),
        grid=(m // tm,),
        in_specs=in_specs,
        out_specs=out_specs,
        out_shape=out_shape,
        compiler_params=_cparams("arbitrary"),
        name="matmul_out",
    )(*args)


def _attn_kernel(slope_ref, lam_ref, qn_ref, kn_ref, q_ref, k_ref, v_ref, g_ref, sg_ref, o_ref,
                 acc_ref, l_ref, s_ref, *, tq, out_scale):
    h = pl.program_id(0)
    i = pl.program_id(1)
    dk = DA_QK_DIM
    slope = slope_ref[h]
    lam = lam_ref[0]
    lanes = LANES
    q0 = i * tq

    def scores(mi, k0, tk, m_old, diag_col):
        colf = lax.broadcasted_iota(jnp.int32, (1, tk), 1).astype(F32)
        bias = slope * (colf + (k0 - q0).astype(F32))
        qm = q_ref[:, mi * dk:(mi + 1) * dk]
        km = k_ref[pl.ds(k0, tk), mi * dk:(mi + 1) * dk]
        s = lax.dot_general(qm, km, (((1,), (1,)), ((), ())), preferred_element_type=F32) + bias
        if diag_col is not None:
            r = lax.broadcasted_iota(jnp.int32, (tq, tk), 0)
            c = lax.broadcasted_iota(jnp.int32, (tq, tk), 1)
            s = jnp.where(r + diag_col >= c, s, -jnp.inf)
        s_ref[mi, :, 0:tk] = s
        smax = jnp.max(s, axis=-1, keepdims=True)
        return smax if m_old is None else jnp.maximum(m_old, smax)

    def accumulate(mi, k0, tk, m_old, m_new):
        vb = v_ref[pl.ds(k0, tk), :]
        p = jnp.exp2(s_ref[mi, :, 0:tk] - m_new)
        psum = p[:, 0:lanes]
        for t in range(1, tk // lanes):
            psum = psum + p[:, t * lanes:(t + 1) * lanes]
        if m_old is None:
            l_ref[mi] = psum
            acc_ref[mi] = jnp.dot(p.astype(BF16), vb, preferred_element_type=F32)
        else:
            alpha = jnp.exp2(m_old - m_new)
            l_ref[mi] = alpha * l_ref[mi] + psum
            acc_ref[mi] = alpha * acc_ref[mi] + jnp.dot(p.astype(BF16), vb, preferred_element_type=F32)

    def tile(k0, tk, m, diag_col=None):
        k0 = pl.multiple_of(k0, tq)
        old = (None, None) if m is None else m
        new = tuple(scores(mi, k0, tk, old[mi], diag_col) for mi in range(2))
        for mi in range(2):
            accumulate(mi, k0, tk, old[mi], new[mi])
        return new

    odd = i % 2 == 1
    m = lax.cond(odd, lambda: tile(q0 - tq, 2 * tq, None, diag_col=tq), lambda: tile(q0, tq, None, diag_col=0))

    qn = [qn_ref[2 * h + mi, i] * ATT_NORM_MARGIN for mi in range(2)]
    floor = [jnp.min(m[mi]) - ATT_SKIP_LOG2 for mi in range(2)]

    def dead(k0, tk):
        b0 = k0 // tq
        top_bias = slope * (k0 + (tk - 1) - q0).astype(F32)
        ok = None
        for mi in range(2):
            kmax = kn_ref[2 * h + mi, b0]
            for t in range(1, tk // tq):
                kmax = jnp.maximum(kmax, kn_ref[2 * h + mi, b0 + t])
            below = qn[mi] * kmax + top_bias < floor[mi]
            ok = below if ok is None else jnp.logical_and(ok, below)
        return ok

    def visit(k0, w, m, take=True):
        half = w // 2
        near_only = functools.partial(lambda m, k0, half: tile(k0 + half, half, m), k0=k0, half=half)
        whole = functools.partial(lambda m, k0, w: tile(k0, w, m), k0=k0, w=w)
        live = jnp.logical_and(take, jnp.logical_not(dead(k0, w)))
        return lax.cond(live, lambda m: lax.cond(dead(k0, half), near_only, whole, m), lambda m: m, m)

    wide = s_ref.shape[2]
    w = 2 * tq
    end = q0 - jnp.where(odd, tq, 0)
    while w < wide:
        take = (i // (w // tq)) % 2 == 1
        m = visit(jnp.where(take, end - w, 0), w, m, take)
        end = end - jnp.where(take, w, 0)
        w *= 2
    lax.fori_loop(0, i // (wide // tq), lambda jj, m: visit(jj * wide, wide, m), m)

    inv1 = 1.0 / jnp.sum(l_ref[0], axis=-1, keepdims=True)
    inv2 = lam / jnp.sum(l_ref[1], axis=-1, keepdims=True)
    o = acc_ref[0] * inv1 - acc_ref[1] * inv2
    y = o * lax.rsqrt(jnp.mean(o * o, axis=-1, keepdims=True) + EPS) * sg_ref[...] * out_scale
    o_ref[...] = (y * _silu(g_ref[...].astype(F32))).astype(o_ref.dtype)


def _block_norm_kernel(x_ref, g_ref, o_ref):
    x = x_ref[...]
    rs = jnp.dot(x * x, g_ref[...], preferred_element_type=F32)
    o_ref[0] = jnp.sqrt(jnp.max(rs, axis=0, keepdims=True))


def _qk_block_norms(u, *, tq):
    s = u.shape[0]
    groups = 4 * DA_HEADS
    cols = groups * DA_QK_DIM
    assert groups <= LANES
    ind = np.zeros((cols, LANES), np.float32)
    ind[np.arange(cols), np.arange(cols) // DA_QK_DIM] = 1.0
    out = pl.pallas_call(
        _block_norm_kernel,
        grid=(s // tq,),
        in_specs=[pl.BlockSpec((tq, cols), lambda i: (i, 0)), pl.BlockSpec((cols, LANES), lambda i: (0, 0))],
        out_specs=pl.BlockSpec((1, 1, LANES), lambda i: (i, 0, 0)),
        out_shape=jax.ShapeDtypeStruct((s // tq, 1, LANES), F32),
        compiler_params=_cparams("arbitrary"),
        name="qk_block_norms",
    )(u, jnp.asarray(ind, BF16))
    return out[:, 0, :groups].T


def _attention(u, slopes, lam, subln_g, *, out_scale, tq=512, tk_wide=2048):
    s = u.shape[0]
    dv = DA_V_DIM
    nh = DA_HEADS
    smem = pl.BlockSpec(memory_space=pltpu.SMEM)
    norms = _qk_block_norms(u, tq=tq)
    return pl.pallas_call(
        functools.partial(_attn_kernel, tq=tq, out_scale=out_scale),
        grid=(nh, s // tq),
        in_specs=[
            smem, smem, smem, smem,
            pl.BlockSpec((tq, dv), lambda h, i: (i, h)),
            pl.BlockSpec((s, dv), lambda h, i: (0, nh + h)),
            pl.BlockSpec((s, dv), lambda h, i: (0, 2 * nh + h)),
            pl.BlockSpec((tq, dv), lambda h, i: (i, 3 * nh + h)),
            pl.BlockSpec((1, dv), lambda h, i: (0, 0)),
        ],
        out_specs=pl.BlockSpec((tq, dv), lambda h, i: (i, h)),
        out_shape=jax.ShapeDtypeStruct((s, nh * dv), BF16),
        scratch_shapes=[pltpu.VMEM((2, tq, dv), F32), pltpu.VMEM((2, tq, LANES), F32),
                        pltpu.VMEM((2, tq, tk_wide), F32)],
        compiler_params=_cparams("arbitrary", "arbitrary"),
        name="diff_attention",
    )(slopes, lam, norms[:2 * nh], norms[2 * nh:], u, u, u, u, subln_g)


def _hgrn_exponent_matrix():
    c = HG_CHUNK
    w = np.zeros((len(HG_LEVELS) + 1, c, c), np.float32)
    for r in range(c):
        w[0, r, :r + 1] = 1.0
        for li, m in enumerate(HG_LEVELS):
            mid = (r // (2 * m)) * 2 * m + m
            if r % (2 * m) >= m:
                w[li + 1, r, mid:r + 1] = 1.0
            else:
                w[li + 1, r, r + 1:mid] = 1.0
    return w.reshape(-1, c)


def _hgrn_kernel(q_ref, f_ref, i_ref, lb_ref, w_ref, o_ref, st_ref, st0_ref, *, tt):
    c = HG_CHUNK

    @pl.when(pl.program_id(1) == 0)
    def _():
        st_ref[...] = jnp.zeros_like(st_ref)

    lb = lb_ref[...]
    wall = w_ref[...]
    row = lax.broadcasted_iota(jnp.int32, (c, 1), 0)
    r2 = lax.broadcasted_iota(jnp.int32, (c, c), 0)
    c2 = lax.broadcasted_iota(jnp.int32, (c, c), 1)
    row8 = lax.broadcasted_iota(jnp.int32, (HG_LEVELS[-1], 1), 0)
    nt = (((1,), (1,)), ((), ()))
    tn = (((0,), (0,)), ((), ()))

    def chunk(ci, carry):
        r0 = pl.multiple_of(ci * c, c)
        q = _silu(q_ref[pl.ds(r0, c), :])
        v = i_ref[pl.ds(r0, c), :]
        fg = lb + (1.0 - lb) * jax.nn.sigmoid(f_ref[pl.ds(r0, c), :])
        kk = 1.0 - fg
        g = jnp.log(fg)
        g_hi = g.astype(BF16)
        g_lo = (g - g_hi.astype(F32)).astype(BF16)
        ex = (jnp.dot(wall, g_hi, preferred_element_type=F32)
              + jnp.dot(wall, g_lo, preferred_element_type=F32))
        b = ex[0:c]
        st = st_ref[...]
        vb = v.astype(BF16)

        o = lax.dot_general((q * jnp.exp(b)).astype(BF16), st.astype(BF16), nt,
                            preferred_element_type=F32)

        a = jnp.zeros((c, c), F32)
        for li, m in enumerate(HG_LEVELS):
            e = jnp.exp(ex[(li + 1) * c:(li + 2) * c])
            upper = (row % (2 * m)) >= m
            qt = jnp.where(upper, q * e, 0.0).astype(BF16)
            kt = jnp.where(upper, 0.0, kk * e).astype(BF16)
            al = lax.dot_general(qt, kt, nt, preferred_element_type=F32)
            if 2 * m != c:
                al = jnp.where((r2 // (2 * m)) == (c2 // (2 * m)), al, 0.0)
            a = a + al
        o = o + jnp.dot(a.astype(BF16), vb, preferred_element_type=F32)

        parts = []
        grp = HG_LEVELS[-1]
        for blk in range(c // grp):
            sl = slice(grp * blk, grp * (blk + 1))
            bb, kb, v8, q8 = b[sl], kk[sl], v[sl], q[sl]
            od = jnp.zeros((grp, HG_DIM), F32)
            for s in range(grp):
                e = jnp.exp(jnp.where(row8 >= s, bb - bb[s:s + 1], -jnp.inf))
                w = jnp.sum(e * q8 * kb[s:s + 1], axis=-1, keepdims=True)
                od = od + w * v8[s:s + 1]
            parts.append(od)
        o_ref[pl.ds(r0, c), :] = o + jnp.concatenate(parts, axis=0)

        b_last = b[c - 1:c]
        kd = (kk * jnp.exp(b_last - b)).astype(BF16)
        st_ref[...] = jnp.exp(b_last) * st + lax.dot_general(vb, kd, tn, preferred_element_type=F32)
        return carry

    n_chunks = tt // c
    dk = HG_DIM
    st0_ref[...] = st_ref[...]

    def side_by_side(fn):
        return jnp.concatenate([fn(n * c) for n in range(n_chunks)], axis=1)

    fg = side_by_side(lambda r0: lb + (1.0 - lb) * jax.nn.sigmoid(f_ref[r0:r0 + c, :]))
    g = jnp.log(fg)
    g_hi = g.astype(BF16)
    g_lo = (g - g_hi.astype(F32)).astype(BF16)
    tril = wall[0:c]
    b = (jnp.dot(tril, g_hi, preferred_element_type=F32)
         + jnp.dot(tril, g_lo, preferred_element_type=F32))
    safe = jnp.min(b[c - 1:c, :]) >= -HG_SAFE_EXP
    eb = jnp.exp(b)
    eb_last = eb[c - 1:c, :]
    qb = (side_by_side(lambda r0: _silu(q_ref[r0:r0 + c, :])) * eb).astype(BF16)
    ke = (1.0 - fg) * jnp.exp(-b)
    keb = ke.astype(BF16)
    kd = (ke * eb_last).astype(BF16)
    vbs = [i_ref[n * c:(n + 1) * c, :].astype(BF16) for n in range(n_chunks)]
    lanes_of = lambda x, n: x[:, n * dk:(n + 1) * dk]
    causal = r2 >= c2
    a = [jnp.where(causal, lax.dot_general(lanes_of(qb, n), lanes_of(keb, n), nt, preferred_element_type=F32),
                   0.0).astype(BF16) for n in range(n_chunks)]
    o_intra = [jnp.dot(a[n], vbs[n], preferred_element_type=F32) for n in range(n_chunks)]
    upd = [lax.dot_general(vbs[n], lanes_of(kd, n), tn, preferred_element_type=F32) for n in range(n_chunks)]
    st = st_ref[...]
    for n in range(n_chunks):
        o_ref[n * c:(n + 1) * c, :] = o_intra[n] + lax.dot_general(
            lanes_of(qb, n), st.astype(BF16), nt, preferred_element_type=F32)
        st = lanes_of(eb_last, n) * st + upd[n]
    st_ref[...] = st

    @pl.when(jnp.logical_not(safe))
    def _():
        st_ref[...] = st0_ref[...]
        lax.fori_loop(0, n_chunks, chunk, 0, unroll=2)


def _hgrn_core(u, lb, *, tt=2048):
    s = u.shape[0]
    nh, dk = HG_HEADS, HG_DIM
    wall = jnp.asarray(_hgrn_exponent_matrix(), BF16)
    return pl.pallas_call(
        functools.partial(_hgrn_kernel, tt=tt),
        grid=(nh, s // tt),
        in_specs=[
            pl.BlockSpec((tt, dk), lambda h, t: (t, h)),
            pl.BlockSpec((tt, dk), lambda h, t: (t, nh + h)),
            pl.BlockSpec((tt, dk), lambda h, t: (t, 2 * nh + h)),
            pl.BlockSpec((1, dk), lambda h, t: (0, h)),
            pl.BlockSpec(wall.shape, lambda h, t: (0, 0)),
        ],
        out_specs=pl.BlockSpec((tt, dk), lambda h, t: (t, h)),
        out_shape=jax.ShapeDtypeStruct((s, nh * dk), F32),
        scratch_shapes=[pltpu.VMEM((dk, dk), F32), pltpu.VMEM((dk, dk), F32)],
        compiler_params=_cparams("arbitrary", "arbitrary"),
        name="hgrn2_core",
    )(u, u, u, lb, wall)


def _conv_kernel(a_ref, ag_ref, gate_ref, dw_ref, dwb_ref, lng_ref, lnb_ref, o_ref, ybuf, cbuf, shbuf,
                 *, tt, cw):
    halo = CONV_HALO
    d = a_ref.shape[1]

    @pl.when(pl.program_id(0) == 0)
    def _():
        ybuf[0:halo, :] = jnp.zeros((halo, d), F32)

    @pl.when(pl.program_id(0) > 0)
    def _():
        ybuf[0:halo, :] = ybuf[tt:tt + halo, :]

    strip = 16

    def glu(si, carry):
        r = pl.multiple_of(si * strip, strip)
        ybuf[pl.ds(pl.multiple_of(halo + r, strip), strip), :] = (
            a_ref[pl.ds(r, strip), :] * _sigmoid(ag_ref[pl.ds(r, strip), :]))
        return carry

    lax.fori_loop(0, tt // strip, glu, 0)
    first = halo - (CONV_WIDTH - 1)

    def lanes(ci, carry):
        c0 = pl.multiple_of(ci * cw, cw)
        acc = jnp.broadcast_to(dwb_ref[:, pl.ds(c0, cw)], (tt, cw))
        for rho in range(SUBLANES):
            taps = [k for k in range(CONV_WIDTH) if (first + k) % SUBLANES == rho]
            if not taps:
                continue
            span = tt + SUBLANES * max((first + k) // SUBLANES for k in taps)
            if rho:
                shbuf[rho, 0:span, :] = ybuf[rho:rho + span, pl.ds(c0, cw)]
            for k in taps:
                a8 = SUBLANES * ((first + k) // SUBLANES)
                yk = shbuf[rho, a8:a8 + tt, :] if rho else ybuf[a8:a8 + tt, pl.ds(c0, cw)]
                acc = acc + yk * dw_ref[k:k + 1, pl.ds(c0, cw)]
        cbuf[:, pl.ds(c0, cw)] = acc
        return carry

    lax.fori_loop(0, d // cw, lanes, 0)

    def layernorm_gate(si, carry):
        r = pl.multiple_of(si * strip, strip)
        y = cbuf[pl.ds(r, strip), :]
        mu = jnp.mean(y, axis=-1, keepdims=True)
        yc = y - mu
        var = jnp.mean(yc * yc, axis=-1, keepdims=True)
        z = yc * lax.rsqrt(var + EPS) * lng_ref[...] + lnb_ref[...]
        o_ref[pl.ds(r, strip), :] = (_silu(z) * _silu(gate_ref[pl.ds(r, strip), :])).astype(o_ref.dtype)
        return carry

    lax.fori_loop(0, tt // strip, layernorm_gate, 0, unroll=4)


def _conv_core(u, dw, dw_b, ln_g, ln_b, *, tt=256, cw=LANES):
    s = u.shape[0]
    d = dw.shape[1]
    row = pl.BlockSpec((1, d), lambda t: (0, 0))
    return pl.pallas_call(
        functools.partial(_conv_kernel, tt=tt, cw=cw),
        grid=(s // tt,),
        in_specs=[
            pl.BlockSpec((tt, d), lambda t: (t, 0)),
            pl.BlockSpec((tt, d), lambda t: (t, 1)),
            pl.BlockSpec((tt, d), lambda t: (t, 2)),
            pl.BlockSpec(dw.shape, lambda t: (0, 0)),
            row, row, row,
        ],
        out_specs=pl.BlockSpec((tt, d), lambda t: (t, 0)),
        out_shape=jax.ShapeDtypeStruct((s, d), BF16),
        scratch_shapes=[pltpu.VMEM((tt + CONV_HALO, d), F32), pltpu.VMEM((tt, d), F32),
                        pltpu.VMEM((SUBLANES, tt + CONV_HALO, cw), F32)],
        compiler_params=_cparams("arbitrary"),
        name="conv_module_core",
    )(u, u, u, dw, dw_b, ln_g, ln_b)


def kernel(x, c, norm_g, ada_w, ada_b, attn_w_in, attn_w_out, attn_lam_q1, attn_lam_k1, attn_lam_q2, attn_lam_k2, attn_subln_g, hgrn_w_in, hgrn_w_out, hgrn_lb_logits, hgrn_gnorm_g, conv_w_in, conv_b_in, conv_dw, conv_dw_b, conv_ln_g, conv_ln_b, conv_w_out, conv_b_out, final_g):
    bsz, s, d = x.shape
    depth = norm_g.shape[0]
    assert bsz == 1 and c.shape == (1, d)
    xs = x.reshape(s, d)
    mods = _mods(c, ada_w, ada_b)

    slopes = (2.0 ** (-8.0 * jnp.arange(1, DA_HEADS + 1, dtype=F32) / DA_HEADS)) * LOG2E
    qk_cols = 2 * DA_HEADS * DA_QK_DIM
    attn_colscale = jnp.concatenate(
        [jnp.full((1, qk_cols), DA_QK_DIM ** -0.5 * LOG2E, F32), jnp.ones((1, 3 * qk_cols), F32)], axis=1)
    lb_all = jax.nn.softmax(hgrn_lb_logits.astype(F32), axis=0)
    lb_all = jnp.cumsum(lb_all, axis=0) - lb_all[0]

    def mod_rows(i):
        return tuple(mods[i, :, k * d:(k + 1) * d] for k in range(3))

    shift, scale, gate = mod_rows(0)
    h = _norm(xs, norm_g[0:1], scale, shift)
    for i in range(depth):
        kind, j = i % N_MIXERS, i // N_MIXERS
        bias_out = pre = None
        if kind == 0:
            u = _matmul(h, attn_w_in, j, out_dtype=BF16, colscale=attn_colscale)
            lam_init = 0.8 - 0.6 * math.exp(-0.3 * i)
            lam = (jnp.exp(jnp.sum(attn_lam_q1[j] * attn_lam_k1[j]))
                   - jnp.exp(jnp.sum(attn_lam_q2[j] * attn_lam_k2[j])) + lam_init).reshape(1)
            y = _attention(u, slopes, lam, attn_subln_g[j:j + 1], out_scale=1.0 - lam_init)
            w_out = attn_w_out[j]
        elif kind == 1:
            u = _matmul(h, hgrn_w_in, j, out_dtype=F32)
            y = _hgrn_core(u, lb_all[i:i + 1])
            pre = (u, u.shape[1] // d - 1, hgrn_gnorm_g[j:j + 1])
            w_out = hgrn_w_out[j]
        else:
            u = _matmul(h, conv_w_in, j, out_dtype=F32, bias=conv_b_in[j:j + 1])
            y = _conv_core(u, conv_dw[j], conv_dw_b[j:j + 1], conv_ln_g[j:j + 1], conv_ln_b[j:j + 1])
            w_out = conv_w_out[j]
            bias_out = conv_b_out[j:j + 1]
        if i + 1 < depth:
            shift_n, scale_n, gate_n = mod_rows(i + 1)
            xs, h = _outproj(y, w_out.astype(BF16), xs, gate, bias_out, norm_g[i + 1:i + 2], scale_n, shift_n,
                             modulate=True, write_x=True, out_dtype=BF16, pre=pre)
            gate = gate_n
        else:
            zero = jnp.zeros((1, d), F32)
            out, = _outproj(y, w_out.astype(BF16), xs, gate, bias_out, final_g.reshape(1, d), zero, zero,
                            modulate=False, write_x=False, out_dtype=F32, pre=pre)
    return out.reshape(bsz, s, d)
```

```python
import functools
import math

import numpy as np
import jax
import jax.numpy as jnp
from jax import lax
from jax.experimental import pallas as pl
from jax.experimental.pallas import tpu as pltpu

F32 = jnp.float32
BF16 = jnp.bfloat16

EPS = 1e-6
LOG2E = 1.4426950408889634
N_MIXERS = 3

DA_HEADS = 8
DA_QK_DIM = 128
DA_V_DIM = 256
ATT_SKIP_LOG2 = 160.0
ATT_NORM_MARGIN = 1.01
HG_HEADS = 16
HG_DIM = 128
HG_CHUNK = 64
HG_LEVELS = (32, 16, 8)
HG_SAFE_EXP = 80.0
CONV_WIDTH = 31
CONV_HALO = 32

SUBLANES = 8
LANES = 128
VMEM_LIMIT_BYTES = 56 * 1024 * 1024


def _cparams(*sem):
    return pltpu.CompilerParams(dimension_semantics=sem, vmem_limit_bytes=VMEM_LIMIT_BYTES)


def _sigmoid(x):
    return 0.5 * jnp.tanh(0.5 * x) + 0.5


def _silu(x):
    h = 0.5 * x
    return h + h * jnp.tanh(h)


def _mods_kernel(c_ref, w_ref, b_ref, o_ref, ca_ref, acc_ref, *, kc, ln):
    kstep = pl.program_id(1)
    kb, n = w_ref.shape[1], w_ref.shape[2]
    ca_ref[...] = _silu(c_ref[...])

    @pl.when(kstep == 0)
    def _():
        acc_ref[...] = jnp.zeros_like(acc_ref)

    for lb in range(n // ln):
        cols = slice(lb * ln, (lb + 1) * ln)

        def body(i, acc):
            k0 = pl.multiple_of(i * kc, kc)
            prod = w_ref[0, pl.ds(k0, kc), cols] * ca_ref[pl.ds(k0, kc), :]
            return acc + prod.reshape(kc // SUBLANES, SUBLANES, ln).sum(axis=0)

        acc_ref[:, cols] = lax.fori_loop(0, kb // kc, body, acc_ref[:, cols], unroll=4)

    @pl.when(kstep == pl.num_programs(1) - 1)
    def _():
        o_ref[0] = jnp.sum(acc_ref[...], axis=0, keepdims=True) + b_ref[0]


def _mods(c, ada_w, ada_b, *, kb=512, kc=32, ln=1024):
    depth, d, n = ada_w.shape
    return pl.pallas_call(
        functools.partial(_mods_kernel, kc=kc, ln=ln),
        grid=(depth, d // kb),
        in_specs=[
            pl.BlockSpec((kb, 1), lambda i, k: (k, 0)),
            pl.BlockSpec((1, kb, n), lambda i, k: (i, k, 0)),
            pl.BlockSpec((1, 1, n), lambda i, k: (i, 0, 0)),
        ],
        out_specs=pl.BlockSpec((1, 1, n), lambda i, k: (i, 0, 0)),
        out_shape=jax.ShapeDtypeStruct((depth, 1, n), F32),
        scratch_shapes=[pltpu.VMEM((kb, 1), F32), pltpu.VMEM((SUBLANES, n), F32)],
        compiler_params=_cparams("arbitrary", "arbitrary"),
        name="adaln_mods",
    )(c.reshape(d, 1), ada_w, ada_b.reshape(depth, 1, n))


def _norm_kernel(x_ref, g_ref, sc_ref, sh_ref, o_ref):
    x = x_ref[...]
    y = x * lax.rsqrt(jnp.mean(x * x, axis=-1, keepdims=True) + EPS) * g_ref[...]
    o_ref[...] = (y * (1.0 + sc_ref[...]) + sh_ref[...]).astype(o_ref.dtype)


def _norm(x, gain, scale, shift, *, tm=512):
    s, d = x.shape
    row = pl.BlockSpec((1, d), lambda i: (0, 0))
    return pl.pallas_call(
        _norm_kernel,
        grid=(s // tm,),
        in_specs=[pl.BlockSpec((tm, d), lambda i: (i, 0)), row, row, row],
        out_specs=pl.BlockSpec((tm, d), lambda i: (i, 0)),
        out_shape=jax.ShapeDtypeStruct((s, d), BF16),
        compiler_params=_cparams("arbitrary"),
        name="rmsnorm_mod",
    )(x, gain, scale, shift)


def _mm_kernel(*refs, has_bias, has_colscale):
    a_ref, w_ref = refs[0], refs[1]
    o_ref, wb_ref = refs[-2], refs[-1]
    pos = 2

    @pl.when(pl.program_id(1) == 0)
    def _():
        wb_ref[...] = w_ref[...].astype(BF16)

    acc = jnp.dot(a_ref[...], wb_ref[...], preferred_element_type=F32)
    if has_bias:
        acc = acc + refs[pos][...]
        pos += 1
    if has_colscale:
        acc = acc * refs[pos][...]
    o_ref[...] = acc.astype(o_ref.dtype)


def _matmul(a, w, layer, *, out_dtype, bias=None, colscale=None, tm=1024, tn=1024):
    m, k = a.shape
    n_out = w.shape[2]
    in_specs = [
        pl.BlockSpec((tm, k), lambda j, i: (i, 0)),
        pl.BlockSpec((None, k, tn), lambda j, i: (layer, 0, j)),
    ]
    args = [a, w]
    rowspec = pl.BlockSpec((1, tn), lambda j, i: (0, j))
    for extra in (bias, colscale):
        if extra is not None:
            in_specs.append(rowspec)
            args.append(extra)
    return pl.pallas_call(
        functools.partial(_mm_kernel, has_bias=bias is not None, has_colscale=colscale is not None),
        grid=(n_out // tn, m // tm),
        in_specs=in_specs,
        out_specs=pl.BlockSpec((tm, tn), lambda j, i: (i, j)),
        out_shape=jax.ShapeDtypeStruct((m, n_out), out_dtype),
        scratch_shapes=[pltpu.VMEM((k, tn), BF16)],
        compiler_params=_cparams("arbitrary", "arbitrary"),
        name="matmul_in",
    )(*args)


def _outproj_kernel(*refs, has_bias, has_pre, modulate, write_x):
    a_ref, w_ref, x_ref, gate_ref = refs[:4]
    pos = 4
    a = a_ref[...]
    if has_pre:
        a = a * lax.rsqrt(jnp.mean(a * a, axis=-1, keepdims=True) + EPS) * refs[pos + 1][...]
        a = (a * _silu(refs[pos][...])).astype(BF16)
        pos += 2
    y = jnp.dot(a, w_ref[...], preferred_element_type=F32)
    if has_bias:
        y = y + refs[pos][...]
        pos += 1
    g_ref, sc_ref, sh_ref = refs[pos:pos + 3]
    xn = x_ref[...] + gate_ref[...] * y
    if write_x:
        refs[-2][...] = xn
    hn = xn * lax.rsqrt(jnp.mean(xn * xn, axis=-1, keepdims=True) + EPS) * g_ref[...]
    if modulate:
        hn = hn * (1.0 + sc_ref[...]) + sh_ref[...]
    refs[-1][...] = hn.astype(refs[-1].dtype)


def _outproj(a, w, x, gate, bias, gain, scale, shift, *, modulate, write_x, out_dtype, pre=None, tm=None):
    m, k = a.shape
    d = w.shape[1]
    if tm is None:
        tm = 256 if pre is not None else 512
    row = pl.BlockSpec((1, d), lambda i: (0, 0))
    tile = pl.BlockSpec((tm, d), lambda i: (i, 0))
    in_specs = [pl.BlockSpec((tm, k), lambda i: (i, 0)), pl.BlockSpec((k, d), lambda i: (0, 0)), tile, row]
    args = [a, w, x, gate]
    if pre is not None:
        u, col_block, pre_gain = pre
        in_specs += [pl.BlockSpec((tm, k), lambda i: (i, col_block)), pl.BlockSpec((1, k), lambda i: (0, 0))]
        args += [u, pre_gain]
    if bias is not None:
        in_specs.append(row)
        args.append(bias)
    in_specs += [row, row, row]
    args += [gain, scale, shift]
    out_specs = [tile]
    out_shape = [jax.ShapeDtypeStruct((m, d), out_dtype)]
    if write_x:
        out_specs = [tile, tile]
        out_shape = [jax.ShapeDtypeStruct((m, d), F32)] + out_shape
    return pl.pallas_call(
        functools.partial(_outproj_kernel, has_bias=bias is not None, has_pre=pre is not None,
                          modulate=modulate, write_x=write_x),
        grid=(m // tm,),
        in_specs=in_specs,
        out_specs=out_specs,
        out_shape=out_shape,
        compiler_params=_cparams("arbitrary"),
        name="matmul_out",
    )(*args)


def _attn_kernel(slope_ref, lam_ref, qn_ref, kn_ref, q_ref, k_ref, v_ref, g_ref, sg_ref, o_ref,
                 acc_ref, l_ref, s_ref, *, tq, out_scale):
    h = pl.program_id(0)
    dk = DA_QK_DIM
    slope = slope_ref[h]
    lam = lam_ref[0]
    lanes = LANES

    def scores(mi, k0, tk, m_old, diag_col):
        colf = lax.broadcasted_iota(jnp.int32, (1, tk), 1).astype(F32)
        bias = slope * (colf + (k0 - q0).astype(F32))
        qm = q_ref[rows, mi * dk:(mi + 1) * dk]
        km = k_ref[pl.ds(k0, tk), mi * dk:(mi + 1) * dk]
        s = lax.dot_general(qm, km, (((1,), (1,)), ((), ())), preferred_element_type=F32) + bias
        if diag_col is not None:
            r = lax.broadcasted_iota(jnp.int32, (tq, tk), 0)
            c = lax.broadcasted_iota(jnp.int32, (tq, tk), 1)
            s = jnp.where(r + diag_col >= c, s, -jnp.inf)
        s_ref[mi, :, 0:tk] = s
        smax = jnp.max(s, axis=-1, keepdims=True)
        return smax if m_old is None else jnp.maximum(m_old, smax)

    def accumulate(mi, k0, tk, m_old, m_new):
        vb = v_ref[pl.ds(k0, tk), :]
        p = jnp.exp2(s_ref[mi, :, 0:tk] - m_new)
        psum = p[:, 0:lanes]
        for t in range(1, tk // lanes):
            psum = psum + p[:, t * lanes:(t + 1) * lanes]
        if m_old is None:
            l_ref[sub, mi] = psum
            acc_ref[sub, mi] = jnp.dot(p.astype(BF16), vb, preferred_element_type=F32)
        else:
            alpha = jnp.exp2(m_old - m_new)
            l_ref[sub, mi] = alpha * l_ref[sub, mi] + psum
            acc_ref[sub, mi] = alpha * acc_ref[sub, mi] + jnp.dot(p.astype(BF16), vb,
                                                                    preferred_element_type=F32)

    def tile(k0, tk, m, diag_col=None):
        k0 = pl.multiple_of(k0, tq)
        old = (None, None) if m is None else m
        new = tuple(scores(mi, k0, tk, old[mi], diag_col) for mi in range(2))
        for mi in range(2):
            accumulate(mi, k0, tk, old[mi], new[mi])
        return new

    def dead(k0, tk):
        b0 = k0 // tq
        top_bias = slope * (k0 + (tk - 1) - q0).astype(F32)
        ok = None
        for mi in range(2):
            kmax = kn_ref[2 * h + mi, b0]
            for t in range(1, tk // tq):
                kmax = jnp.maximum(kmax, kn_ref[2 * h + mi, b0 + t])
            below = qn[mi] * kmax + top_bias < floor[mi]
            ok = below if ok is None else jnp.logical_and(ok, below)
        return ok

    def visit(k0, w, m, take=True):
        half = w // 2
        near_only = functools.partial(lambda m, k0, half: tile(k0 + half, half, m), k0=k0, half=half)
        whole = functools.partial(lambda m, k0, w: tile(k0, w, m), k0=k0, w=w)
        live = jnp.logical_and(take, jnp.logical_not(dead(k0, w)))
        return lax.cond(live, lambda m: lax.cond(dead(k0, half), near_only, whole, m), lambda m: m, m)

    wide = s_ref.shape[2]
    for sub in range(2):
        i = 2 * pl.program_id(1) + sub
        q0 = i * tq
        rows = slice(sub * tq, (sub + 1) * tq)
        m = tile(q0 - tq, 2 * tq, None, diag_col=tq) if sub else tile(q0, tq, None, diag_col=0)
        qn = [qn_ref[2 * h + mi, i] * ATT_NORM_MARGIN for mi in range(2)]
        floor = [jnp.min(m[mi]) - ATT_SKIP_LOG2 for mi in range(2)]
        w = 2 * tq
        end = q0 - sub * tq
        while w < wide:
            take = (i // (w // tq)) % 2 == 1
            m = visit(jnp.where(take, end - w, 0), w, m, take)
            end = end - jnp.where(take, w, 0)
            w *= 2
        lax.fori_loop(0, i // (wide // tq), lambda jj, m: visit(jj * wide, wide, m), m)

        inv1 = 1.0 / jnp.sum(l_ref[sub, 0], axis=-1, keepdims=True)
        inv2 = lam / jnp.sum(l_ref[sub, 1], axis=-1, keepdims=True)
        o = acc_ref[sub, 0] * inv1 - acc_ref[sub, 1] * inv2
        y = o * lax.rsqrt(jnp.mean(o * o, axis=-1, keepdims=True) + EPS) * sg_ref[...] * out_scale
        o_ref[rows, :] = (y * _silu(g_ref[rows, :].astype(F32))).astype(o_ref.dtype)


def _block_norm_kernel(x_ref, g_ref, o_ref):
    x = x_ref[...]
    rs = jnp.dot(x * x, g_ref[...], preferred_element_type=F32)
    o_ref[0] = jnp.sqrt(jnp.max(rs, axis=0, keepdims=True))


def _qk_block_norms(u, *, tq):
    s = u.shape[0]
    groups = 4 * DA_HEADS
    cols = groups * DA_QK_DIM
    assert groups <= LANES
    ind = np.zeros((cols, LANES), np.float32)
    ind[np.arange(cols), np.arange(cols) // DA_QK_DIM] = 1.0
    out = pl.pallas_call(
        _block_norm_kernel,
        grid=(s // tq,),
        in_specs=[pl.BlockSpec((tq, cols), lambda i: (i, 0)), pl.BlockSpec((cols, LANES), lambda i: (0, 0))],
        out_specs=pl.BlockSpec((1, 1, LANES), lambda i: (i, 0, 0)),
        out_shape=jax.ShapeDtypeStruct((s // tq, 1, LANES), F32),
        compiler_params=_cparams("arbitrary"),
        name="qk_block_norms",
    )(u, jnp.asarray(ind, BF16))
    return out[:, 0, :groups].T


def _attention(u, slopes, lam, subln_g, *, out_scale, tq=512, tk_wide=2048):
    s = u.shape[0]
    dv = DA_V_DIM
    nh = DA_HEADS
    smem = pl.BlockSpec(memory_space=pltpu.SMEM)
    norms = _qk_block_norms(u, tq=tq)
    return pl.pallas_call(
        functools.partial(_attn_kernel, tq=tq, out_scale=out_scale),
        grid=(nh, s // (2 * tq)),
        in_specs=[
            smem, smem, smem, smem,
            pl.BlockSpec((2 * tq, dv), lambda h, g: (g, h)),
            pl.BlockSpec((s, dv), lambda h, g: (0, nh + h)),
            pl.BlockSpec((s, dv), lambda h, g: (0, 2 * nh + h)),
            pl.BlockSpec((2 * tq, dv), lambda h, g: (g, 3 * nh + h)),
            pl.BlockSpec((1, dv), lambda h, g: (0, 0)),
        ],
        out_specs=pl.BlockSpec((2 * tq, dv), lambda h, g: (g, h)),
        out_shape=jax.ShapeDtypeStruct((s, nh * dv), BF16),
        scratch_shapes=[pltpu.VMEM((2, 2, tq, dv), F32), pltpu.VMEM((2, 2, tq, LANES), F32),
                        pltpu.VMEM((2, tq, tk_wide), F32)],
        compiler_params=_cparams("arbitrary", "arbitrary"),
        name="diff_attention",
    )(slopes, lam, norms[:2 * nh], norms[2 * nh:], u, u, u, u, subln_g)


def _hgrn_exponent_matrix():
    c = HG_CHUNK
    w = np.zeros((len(HG_LEVELS) + 1, c, c), np.float32)
    for r in range(c):
        w[0, r, :r + 1] = 1.0
        for li, m in enumerate(HG_LEVELS):
            mid = (r // (2 * m)) * 2 * m + m
            if r % (2 * m) >= m:
                w[li + 1, r, mid:r + 1] = 1.0
            else:
                w[li + 1, r, r + 1:mid] = 1.0
    return w.reshape(-1, c)


def _hgrn_kernel(q_ref, f_ref, i_ref, lb_ref, w_ref, o_ref, st_ref, st0_ref, *, tt):
    c = HG_CHUNK

    @pl.when(pl.program_id(1) == 0)
    def _():
        st_ref[...] = jnp.zeros_like(st_ref)

    lb = lb_ref[...]
    wall = w_ref[...]
    row = lax.broadcasted_iota(jnp.int32, (c, 1), 0)
    r2 = lax.broadcasted_iota(jnp.int32, (c, c), 0)
    c2 = lax.broadcasted_iota(jnp.int32, (c, c), 1)
    row8 = lax.broadcasted_iota(jnp.int32, (HG_LEVELS[-1], 1), 0)
    nt = (((1,), (1,)), ((), ()))
    tn = (((0,), (0,)), ((), ()))

    def chunk(ci, carry):
        r0 = pl.multiple_of(ci * c, c)
        q = _silu(q_ref[pl.ds(r0, c), :])
        v = i_ref[pl.ds(r0, c), :]
        fg = lb + (1.0 - lb) * jax.nn.sigmoid(f_ref[pl.ds(r0, c), :])
        kk = 1.0 - fg
        g = jnp.log(fg)
        g_hi = g.astype(BF16)
        g_lo = (g - g_hi.astype(F32)).astype(BF16)
        ex = (jnp.dot(wall, g_hi, preferred_element_type=F32)
              + jnp.dot(wall, g_lo, preferred_element_type=F32))
        b = ex[0:c]
        st = st_ref[...]
        vb = v.astype(BF16)

        o = lax.dot_general((q * jnp.exp(b)).astype(BF16), st.astype(BF16), nt,
                            preferred_element_type=F32)

        a = jnp.zeros((c, c), F32)
        for li, m in enumerate(HG_LEVELS):
            e = jnp.exp(ex[(li + 1) * c:(li + 2) * c])
            upper = (row % (2 * m)) >= m
            qt = jnp.where(upper, q * e, 0.0).astype(BF16)
            kt = jnp.where(upper, 0.0, kk * e).astype(BF16)
            al = lax.dot_general(qt, kt, nt, preferred_element_type=F32)
            if 2 * m != c:
                al = jnp.where((r2 // (2 * m)) == (c2 // (2 * m)), al, 0.0)
            a = a + al
        o = o + jnp.dot(a.astype(BF16), vb, preferred_element_type=F32)

        parts = []
        grp = HG_LEVELS[-1]
        for blk in range(c // grp):
            sl = slice(grp * blk, grp * (blk + 1))
            bb, kb, v8, q8 = b[sl], kk[sl], v[sl], q[sl]
            od = jnp.zeros((grp, HG_DIM), F32)
            for s in range(grp):
                e = jnp.exp(jnp.where(row8 >= s, bb - bb[s:s + 1], -jnp.inf))
                w = jnp.sum(e * q8 * kb[s:s + 1], axis=-1, keepdims=True)
                od = od + w * v8[s:s + 1]
            parts.append(od)
        o_ref[pl.ds(r0, c), :] = o + jnp.concatenate(parts, axis=0)

        b_last = b[c - 1:c]
        kd = (kk * jnp.exp(b_last - b)).astype(BF16)
        st_ref[...] = jnp.exp(b_last) * st + lax.dot_general(vb, kd, tn, preferred_element_type=F32)
        return carry

    n_chunks = tt // c
    dk = HG_DIM
    st0_ref[...] = st_ref[...]

    def side_by_side(fn):
        return jnp.concatenate([fn(n * c) for n in range(n_chunks)], axis=1)

    fg = side_by_side(lambda r0: lb + (1.0 - lb) * jax.nn.sigmoid(f_ref[r0:r0 + c, :]))
    g = jnp.log(fg)
    g_hi = g.astype(BF16)
    g_lo = (g - g_hi.astype(F32)).astype(BF16)
    tril = wall[0:c]
    b = (jnp.dot(tril, g_hi, preferred_element_type=F32)
         + jnp.dot(tril, g_lo, preferred_element_type=F32))
    safe = jnp.min(b[c - 1:c, :]) >= -HG_SAFE_EXP
    eb = jnp.exp(b)
    eb_last = eb[c - 1:c, :]
    qb = (side_by_side(lambda r0: _silu(q_ref[r0:r0 + c, :])) * eb).astype(BF16)
    ke = (1.0 - fg) * jnp.exp(-b)
    keb = ke.astype(BF16)
    kd = (ke * eb_last).astype(BF16)
    vbs = [i_ref[n * c:(n + 1) * c, :].astype(BF16) for n in range(n_chunks)]
    lanes_of = lambda x, n: x[:, n * dk:(n + 1) * dk]
    causal = r2 >= c2
    a = [jnp.where(causal, lax.dot_general(lanes_of(qb, n), lanes_of(keb, n), nt, preferred_element_type=F32),
                   0.0).astype(BF16) for n in range(n_chunks)]
    o_intra = [jnp.dot(a[n], vbs[n], preferred_element_type=F32) for n in range(n_chunks)]
    upd = [lax.dot_general(vbs[n], lanes_of(kd, n), tn, preferred_element_type=F32) for n in range(n_chunks)]
    st = st_ref[...]
    for n in range(n_chunks):
        o_ref[n * c:(n + 1) * c, :] = o_intra[n] + lax.dot_general(
            lanes_of(qb, n), st.astype(BF16), nt, preferred_element_type=F32)
        st = lanes_of(eb_last, n) * st + upd[n]
    st_ref[...] = st

    @pl.when(jnp.logical_not(safe))
    def _():
        st_ref[...] = st0_ref[...]
        lax.fori_loop(0, n_chunks, chunk, 0, unroll=2)


def _hgrn_core(u, lb, *, tt=2048):
    s = u.shape[0]
    nh, dk = HG_HEADS, HG_DIM
    wall = jnp.asarray(_hgrn_exponent_matrix(), BF16)
    return pl.pallas_call(
        functools.partial(_hgrn_kernel, tt=tt),
        grid=(nh, s // tt),
        in_specs=[
            pl.BlockSpec((tt, dk), lambda h, t: (t, h)),
            pl.BlockSpec((tt, dk), lambda h, t: (t, nh + h)),
            pl.BlockSpec((tt, dk), lambda h, t: (t, 2 * nh + h)),
            pl.BlockSpec((1, dk), lambda h, t: (0, h)),
            pl.BlockSpec(wall.shape, lambda h, t: (0, 0)),
        ],
        out_specs=pl.BlockSpec((tt, dk), lambda h, t: (t, h)),
        out_shape=jax.ShapeDtypeStruct((s, nh * dk), F32),
        scratch_shapes=[pltpu.VMEM((dk, dk), F32), pltpu.VMEM((dk, dk), F32)],
        compiler_params=_cparams("arbitrary", "arbitrary"),
        name="hgrn2_core",
    )(u, u, u, lb, wall)


def _conv_kernel(a_ref, ag_ref, gate_ref, dw_ref, dwb_ref, lng_ref, lnb_ref, o_ref, ybuf, cbuf, shbuf,
                 *, tt, cw):
    halo = CONV_HALO
    d = a_ref.shape[1]

    @pl.when(pl.program_id(0) == 0)
    def _():
        ybuf[0:halo, :] = jnp.zeros((halo, d), F32)

    @pl.when(pl.program_id(0) > 0)
    def _():
        ybuf[0:halo, :] = ybuf[tt:tt + halo, :]

    strip = 16

    def glu(si, carry):
        r = pl.multiple_of(si * strip, strip)
        ybuf[pl.ds(pl.multiple_of(halo + r, strip), strip), :] = (
            a_ref[pl.ds(r, strip), :] * _sigmoid(ag_ref[pl.ds(r, strip), :]))
        return carry

    lax.fori_loop(0, tt // strip, glu, 0)
    first = halo - (CONV_WIDTH - 1)

    def lanes(ci, carry):
        c0 = pl.multiple_of(ci * cw, cw)
        acc = jnp.broadcast_to(dwb_ref[:, pl.ds(c0, cw)], (tt, cw))
        for rho in range(SUBLANES):
            taps = [k for k in range(CONV_WIDTH) if (first + k) % SUBLANES == rho]
            if not taps:
                continue
            span = tt + SUBLANES * max((first + k) // SUBLANES for k in taps)
            if rho:
                shbuf[rho, 0:span, :] = ybuf[rho:rho + span, pl.ds(c0, cw)]
            for k in taps:
                a8 = SUBLANES * ((first + k) // SUBLANES)
                yk = shbuf[rho, a8:a8 + tt, :] if rho else ybuf[a8:a8 + tt, pl.ds(c0, cw)]
                acc = acc + yk * dw_ref[k:k + 1, pl.ds(c0, cw)]
        cbuf[:, pl.ds(c0, cw)] = acc
        return carry

    lax.fori_loop(0, d // cw, lanes, 0)

    def layernorm_gate(si, carry):
        r = pl.multiple_of(si * strip, strip)
        y = cbuf[pl.ds(r, strip), :]
        mu = jnp.mean(y, axis=-1, keepdims=True)
        yc = y - mu
        var = jnp.mean(yc * yc, axis=-1, keepdims=True)
        z = yc * lax.rsqrt(var + EPS) * lng_ref[...] + lnb_ref[...]
        o_ref[pl.ds(r, strip), :] = (_silu(z) * _silu(gate_ref[pl.ds(r, strip), :])).astype(o_ref.dtype)
        return carry

    lax.fori_loop(0, tt // strip, layernorm_gate, 0, unroll=4)


def _conv_core(u, dw, dw_b, ln_g, ln_b, *, tt=256, cw=LANES):
    s = u.shape[0]
    d = dw.shape[1]
    row = pl.BlockSpec((1, d), lambda t: (0, 0))
    return pl.pallas_call(
        functools.partial(_conv_kernel, tt=tt, cw=cw),
        grid=(s // tt,),
        in_specs=[
            pl.BlockSpec((tt, d), lambda t: (t, 0)),
            pl.BlockSpec((tt, d), lambda t: (t, 1)),
            pl.BlockSpec((tt, d), lambda t: (t, 2)),
            pl.BlockSpec(dw.shape, lambda t: (0, 0)),
            row, row, row,
        ],
        out_specs=pl.BlockSpec((tt, d), lambda t: (t, 0)),
        out_shape=jax.ShapeDtypeStruct((s, d), BF16),
        scratch_shapes=[pltpu.VMEM((tt + CONV_HALO, d), F32), pltpu.VMEM((tt, d), F32),
                        pltpu.VMEM((SUBLANES, tt + CONV_HALO, cw), F32)],
        compiler_params=_cparams("arbitrary"),
        name="conv_module_core",
    )(u, u, u, dw, dw_b, ln_g, ln_b)


def kernel(x, c, norm_g, ada_w, ada_b, attn_w_in, attn_w_out, attn_lam_q1, attn_lam_k1, attn_lam_q2, attn_lam_k2, attn_subln_g, hgrn_w_in, hgrn_w_out, hgrn_lb_logits, hgrn_gnorm_g, conv_w_in, conv_b_in, conv_dw, conv_dw_b, conv_ln_g, conv_ln_b, conv_w_out, conv_b_out, final_g):
    bsz, s, d = x.shape
    depth = norm_g.shape[0]
    assert bsz == 1 and c.shape == (1, d)
    xs = x.reshape(s, d)
    mods = _mods(c, ada_w, ada_b)

    slopes = (2.0 ** (-8.0 * jnp.arange(1, DA_HEADS + 1, dtype=F32) / DA_HEADS)) * LOG2E
    qk_cols = 2 * DA_HEADS * DA_QK_DIM
    attn_colscale = jnp.concatenate(
        [jnp.full((1, qk_cols), DA_QK_DIM ** -0.5 * LOG2E, F32), jnp.ones((1, 3 * qk_cols), F32)], axis=1)
    lb_all = jax.nn.softmax(hgrn_lb_logits.astype(F32), axis=0)
    lb_all = jnp.cumsum(lb_all, axis=0) - lb_all[0]

    def mod_rows(i):
        return tuple(mods[i, :, k * d:(k + 1) * d] for k in range(3))

    shift, scale, gate = mod_rows(0)
    h = _norm(xs, norm_g[0:1], scale, shift)
    for i in range(depth):
        kind, j = i % N_MIXERS, i // N_MIXERS
        bias_out = pre = None
        if kind == 0:
            u = _matmul(h, attn_w_in, j, out_dtype=BF16, colscale=attn_colscale)
            lam_init = 0.8 - 0.6 * math.exp(-0.3 * i)
            lam = (jnp.exp(jnp.sum(attn_lam_q1[j] * attn_lam_k1[j]))
                   - jnp.exp(jnp.sum(attn_lam_q2[j] * attn_lam_k2[j])) + lam_init).reshape(1)
            y = _attention(u, slopes, lam, attn_subln_g[j:j + 1], out_scale=1.0 - lam_init)
            w_out = attn_w_out[j]
        elif kind == 1:
            u = _matmul(h, hgrn_w_in, j, out_dtype=F32)
            y = _hgrn_core(u, lb_all[i:i + 1])
            pre = (u, u.shape[1] // d - 1, hgrn_gnorm_g[j:j + 1])
            w_out = hgrn_w_out[j]
        else:
            u = _matmul(h, conv_w_in, j, out_dtype=F32, bias=conv_b_in[j:j + 1])
            y = _conv_core(u, conv_dw[j], conv_dw_b[j:j + 1], conv_ln_g[j:j + 1], conv_ln_b[j:j + 1])
            w_out = conv_w_out[j]
            bias_out = conv_b_out[j:j + 1]
        if i + 1 < depth:
            shift_n, scale_n, gate_n = mod_rows(i + 1)
            xs, h = _outproj(y, w_out.astype(BF16), xs, gate, bias_out, norm_g[i + 1:i + 2], scale_n, shift_n,
                             modulate=True, write_x=True, out_dtype=BF16, pre=pre)
            gate = gate_n
        else:
            zero = jnp.zeros((1, d), F32)
            out, = _outproj(y, w_out.astype(BF16), xs, gate, bias_out, final_g.reshape(1, d), zero, zero,
                            modulate=False, write_x=False, out_dtype=F32, pre=pre)
    return out.reshape(bsz, s, d)
```

```python
import functools
import math

import numpy as np
import jax
import jax.numpy as jnp
from jax import lax
from jax.experimental import pallas as pl
from jax.experimental.pallas import tpu as pltpu

F32 = jnp.float32
BF16 = jnp.bfloat16

EPS = 1e-6
LOG2E = 1.4426950408889634
N_MIXERS = 3

DA_HEADS = 8
DA_QK_DIM = 128
DA_V_DIM = 256
ATT_SKIP_LOG2 = 160.0
ATT_NORM_MARGIN = 1.01
HG_HEADS = 16
HG_DIM = 128
HG_CHUNK = 64
HG_LEVELS = (32, 16, 8)
HG_SAFE_EXP = 80.0
CONV_WIDTH = 31
CONV_HALO = 32

SUBLANES = 8
LANES = 128
VMEM_LIMIT_BYTES = 56 * 1024 * 1024


def _cparams(*sem):
    return pltpu.CompilerParams(dimension_semantics=sem, vmem_limit_bytes=VMEM_LIMIT_BYTES)


def _sigmoid(x):
    return 0.5 * jnp.tanh(0.5 * x) + 0.5


def _silu(x):
    h = 0.5 * x
    return h + h * jnp.tanh(h)


def _mods_kernel(c_ref, w_ref, b_ref, o_ref, ca_ref, acc_ref, *, kc, ln):
    kstep = pl.program_id(1)
    kb, n = w_ref.shape[1], w_ref.shape[2]
    ca_ref[...] = _silu(c_ref[...])

    @pl.when(kstep == 0)
    def _():
        acc_ref[...] = jnp.zeros_like(acc_ref)

    for lb in range(n // ln):
        cols = slice(lb * ln, (lb + 1) * ln)

        def body(i, acc):
            k0 = pl.multiple_of(i * kc, kc)
            prod = w_ref[0, pl.ds(k0, kc), cols] * ca_ref[pl.ds(k0, kc), :]
            return acc + prod.reshape(kc // SUBLANES, SUBLANES, ln).sum(axis=0)

        acc_ref[:, cols] = lax.fori_loop(0, kb // kc, body, acc_ref[:, cols], unroll=4)

    @pl.when(kstep == pl.num_programs(1) - 1)
    def _():
        o_ref[0] = jnp.sum(acc_ref[...], axis=0, keepdims=True) + b_ref[0]


def _mods(c, ada_w, ada_b, *, kb=512, kc=32, ln=1024):
    depth, d, n = ada_w.shape
    return pl.pallas_call(
        functools.partial(_mods_kernel, kc=kc, ln=ln),
        grid=(depth, d // kb),
        in_specs=[
            pl.BlockSpec((kb, 1), lambda i, k: (k, 0)),
            pl.BlockSpec((1, kb, n), lambda i, k: (i, k, 0)),
            pl.BlockSpec((1, 1, n), lambda i, k: (i, 0, 0)),
        ],
        out_specs=pl.BlockSpec((1, 1, n), lambda i, k: (i, 0, 0)),
        out_shape=jax.ShapeDtypeStruct((depth, 1, n), F32),
        scratch_shapes=[pltpu.VMEM((kb, 1), F32), pltpu.VMEM((SUBLANES, n), F32)],
        compiler_params=_cparams("arbitrary", "arbitrary"),
        name="adaln_mods",
    )(c.reshape(d, 1), ada_w, ada_b.reshape(depth, 1, n))


def _norm_kernel(x_ref, g_ref, sc_ref, sh_ref, o_ref):
    x = x_ref[...]
    y = x * lax.rsqrt(jnp.mean(x * x, axis=-1, keepdims=True) + EPS) * g_ref[...]
    o_ref[...] = (y * (1.0 + sc_ref[...]) + sh_ref[...]).astype(o_ref.dtype)


def _norm(x, gain, scale, shift, *, tm=512):
    s, d = x.shape
    row = pl.BlockSpec((1, d), lambda i: (0, 0))
    return pl.pallas_call(
        _norm_kernel,
        grid=(s // tm,),
        in_specs=[pl.BlockSpec((tm, d), lambda i: (i, 0)), row, row, row],
        out_specs=pl.BlockSpec((tm, d), lambda i: (i, 0)),
        out_shape=jax.ShapeDtypeStruct((s, d), BF16),
        compiler_params=_cparams("arbitrary"),
        name="rmsnorm_mod",
    )(x, gain, scale, shift)


def _mm_kernel(*refs, has_bias, has_colscale):
    a_ref, w_ref = refs[0], refs[1]
    o_ref, wb_ref = refs[-2], refs[-1]
    pos = 2

    @pl.when(pl.program_id(1) == 0)
    def _():
        wb_ref[...] = w_ref[...].astype(BF16)

    acc = jnp.dot(a_ref[...], wb_ref[...], preferred_element_type=F32)
    if has_bias:
        acc = acc + refs[pos][...]
        pos += 1
    if has_colscale:
        acc = acc * refs[pos][...]
    o_ref[...] = acc.astype(o_ref.dtype)


def _matmul(a, w, layer, *, out_dtype, bias=None, colscale=None, tm=1024, tn=1024):
    m, k = a.shape
    n_out = w.shape[2]
    in_specs = [
        pl.BlockSpec((tm, k), lambda j, i: (i, 0)),
        pl.BlockSpec((None, k, tn), lambda j, i: (layer, 0, j)),
    ]
    args = [a, w]
    rowspec = pl.BlockSpec((1, tn), lambda j, i: (0, j))
    for extra in (bias, colscale):
        if extra is not None:
            in_specs.append(rowspec)
            args.append(extra)
    return pl.pallas_call(
        functools.partial(_mm_kernel, has_bias=bias is not None, has_colscale=colscale is not None),
        grid=(n_out // tn, m // tm),
        in_specs=in_specs,
        out_specs=pl.BlockSpec((tm, tn), lambda j, i: (i, j)),
        out_shape=jax.ShapeDtypeStruct((m, n_out), out_dtype),
        scratch_shapes=[pltpu.VMEM((k, tn), BF16)],
        compiler_params=_cparams("arbitrary", "arbitrary"),
        name="matmul_in",
    )(*args)


def _outproj_kernel(*refs, has_bias, has_pre, modulate, write_x):
    a_ref, w_ref, x_ref, gate_ref = refs[:4]
    pos = 4
    a = a_ref[...]
    if has_pre:
        a = a * lax.rsqrt(jnp.mean(a * a, axis=-1, keepdims=True) + EPS) * refs[pos + 1][...]
        a = (a * _silu(refs[pos][...])).astype(BF16)
        pos += 2
    y = jnp.dot(a, w_ref[...], preferred_element_type=F32)
    if has_bias:
        y = y + refs[pos][...]
        pos += 1
    g_ref, sc_ref, sh_ref = refs[pos:pos + 3]
    xn = x_ref[...] + gate_ref[...] * y
    if write_x:
        refs[-2][...] = xn
    hn = xn * lax.rsqrt(jnp.mean(xn * xn, axis=-1, keepdims=True) + EPS) * g_ref[...]
    if modulate:
        hn = hn * (1.0 + sc_ref[...]) + sh_ref[...]
    refs[-1][...] = hn.astype(refs[-1].dtype)


def _outproj(a, w, x, gate, bias, gain, scale, shift, *, modulate, write_x, out_dtype, pre=None, tm=None):
    m, k = a.shape
    d = w.shape[1]
    if tm is None:
        tm = 256 if pre is not None else 512
    row = pl.BlockSpec((1, d), lambda i: (0, 0))
    tile = pl.BlockSpec((tm, d), lambda i: (i, 0))
    in_specs = [pl.BlockSpec((tm, k), lambda i: (i, 0)), pl.BlockSpec((k, d), lambda i: (0, 0)), tile, row]
    args = [a, w, x, gate]
    if pre is not None:
        u, col_block, pre_gain = pre
        in_specs += [pl.BlockSpec((tm, k), lambda i: (i, col_block)), pl.BlockSpec((1, k), lambda i: (0, 0))]
        args += [u, pre_gain]
    if bias is not None:
        in_specs.append(row)
        args.append(bias)
    in_specs += [row, row, row]
    args += [gain, scale, shift]
    out_specs = [tile]
    out_shape = [jax.ShapeDtypeStruct((m, d), out_dtype)]
    if write_x:
        out_specs = [tile, tile]
        out_shape = [jax.ShapeDtypeStruct((m, d), F32)] + out_shape
    return pl.pallas_call(
        functools.partial(_outproj_kernel, has_bias=bias is not None, has_pre=pre is not None,
                          modulate=modulate, write_x=write_x),
        grid=(m // tm,),
        in_specs=in_specs,
        out_specs=out_specs,
        out_shape=out_shape,
        compiler_params=_cparams("arbitrary"),
        name="matmul_out",
    )(*args)


def _attn_kernel(slope_ref, lam_ref, qn_ref, kn_ref, q_ref, k_ref, v_ref, g_ref, sg_ref, o_ref,
                 acc_ref, l_ref, s_ref, *, tq, out_scale):
    h = pl.program_id(0)
    dk = DA_QK_DIM
    slope = slope_ref[h]
    lam = lam_ref[0]
    lanes = LANES

    def scores(mi, k0, tk, m_old, diag_col):
        colf = lax.broadcasted_iota(jnp.int32, (1, tk), 1).astype(F32)
        bias = slope * (colf + (k0 - q0).astype(F32))
        qm = q_ref[rows, mi * dk:(mi + 1) * dk]
        km = k_ref[pl.ds(k0, tk), mi * dk:(mi + 1) * dk]
        s = lax.dot_general(qm, km, (((1,), (1,)), ((), ())), preferred_element_type=F32) + bias
        if diag_col is not None:
            r = lax.broadcasted_iota(jnp.int32, (tq, tk), 0)
            c = lax.broadcasted_iota(jnp.int32, (tq, tk), 1)
            s = jnp.where(r + diag_col >= c, s, -jnp.inf)
        s_ref[mi, :, 0:tk] = s
        smax = jnp.max(s, axis=-1, keepdims=True)
        return smax if m_old is None else jnp.maximum(m_old, smax)

    def accumulate(mi, k0, tk, m_old, m_new):
        vb = v_ref[pl.ds(k0, tk), :]
        p = jnp.exp2(s_ref[mi, :, 0:tk] - m_new)
        psum = p[:, 0:lanes]
        for t in range(1, tk // lanes):
            psum = psum + p[:, t * lanes:(t + 1) * lanes]
        if m_old is None:
            l_ref[sub, mi] = psum
            acc_ref[sub, mi] = jnp.dot(p.astype(BF16), vb, preferred_element_type=F32)
        else:
            alpha = jnp.exp2(m_old - m_new)
            l_ref[sub, mi] = alpha * l_ref[sub, mi] + psum
            acc_ref[sub, mi] = alpha * acc_ref[sub, mi] + jnp.dot(p.astype(BF16), vb,
                                                                    preferred_element_type=F32)

    def tile(k0, tk, m, diag_col=None):
        k0 = pl.multiple_of(k0, tq)
        old = (None, None) if m is None else m
        new = tuple(scores(mi, k0, tk, old[mi], diag_col) for mi in range(2))
        for mi in range(2):
            accumulate(mi, k0, tk, old[mi], new[mi])
        return new

    def dead(k0, tk):
        b0 = k0 // tq
        top_bias = slope * (k0 + (tk - 1) - q0).astype(F32)
        ok = None
        for mi in range(2):
            kmax = kn_ref[2 * h + mi, b0]
            for t in range(1, tk // tq):
                kmax = jnp.maximum(kmax, kn_ref[2 * h + mi, b0 + t])
            below = qn[mi] * kmax + top_bias < floor[mi]
            ok = below if ok is None else jnp.logical_and(ok, below)
        return ok

    def visit(k0, w, m):
        half = w // 2
        near_only = functools.partial(lambda m, k0, half: tile(k0 + half, half, m), k0=k0, half=half)
        whole = functools.partial(lambda m, k0, w: tile(k0, w, m), k0=k0, w=w)
        return lax.cond(dead(k0, w), lambda m: m, lambda m: lax.cond(dead(k0, half), near_only, whole, m), m)

    wide = s_ref.shape[2]
    n_sub = wide // tq
    for sub in range(n_sub):
        i = n_sub * pl.program_id(1) + sub
        q0 = i * tq
        rows = slice(sub * tq, (sub + 1) * tq)
        odd = sub % 2
        m = tile(q0 - tq, 2 * tq, None, diag_col=tq) if odd else tile(q0, tq, None, diag_col=0)
        qn = [qn_ref[2 * h + mi, i] * ATT_NORM_MARGIN for mi in range(2)]
        floor = [jnp.min(m[mi]) - ATT_SKIP_LOG2 for mi in range(2)]
        w = 2 * tq
        end = q0 - odd * tq
        while w < wide:
            if (sub // (w // tq)) % 2:
                m = visit(end - w, w, m)
                end = end - w
            w *= 2
        lax.fori_loop(0, pl.program_id(1), lambda jj, m: visit(jj * wide, wide, m), m)

        inv1 = 1.0 / jnp.sum(l_ref[sub, 0], axis=-1, keepdims=True)
        inv2 = lam / jnp.sum(l_ref[sub, 1], axis=-1, keepdims=True)
        o = acc_ref[sub, 0] * inv1 - acc_ref[sub, 1] * inv2
        y = o * lax.rsqrt(jnp.mean(o * o, axis=-1, keepdims=True) + EPS) * sg_ref[...] * out_scale
        o_ref[rows, :] = (y * _silu(g_ref[rows, :].astype(F32))).astype(o_ref.dtype)


def _block_norm_kernel(x_ref, g_ref, o_ref):
    x = x_ref[...]
    rs = jnp.dot(x * x, g_ref[...], preferred_element_type=F32)
    o_ref[0] = jnp.sqrt(jnp.max(rs, axis=0, keepdims=True))


def _qk_block_norms(u, *, tq):
    s = u.shape[0]
    groups = 4 * DA_HEADS
    cols = groups * DA_QK_DIM
    assert groups <= LANES
    ind = np.zeros((cols, LANES), np.float32)
    ind[np.arange(cols), np.arange(cols) // DA_QK_DIM] = 1.0
    out = pl.pallas_call(
        _block_norm_kernel,
        grid=(s // tq,),
        in_specs=[pl.BlockSpec((tq, cols), lambda i: (i, 0)), pl.BlockSpec((cols, LANES), lambda i: (0, 0))],
        out_specs=pl.BlockSpec((1, 1, LANES), lambda i: (i, 0, 0)),
        out_shape=jax.ShapeDtypeStruct((s // tq, 1, LANES), F32),
        compiler_params=_cparams("arbitrary"),
        name="qk_block_norms",
    )(u, jnp.asarray(ind, BF16))
    return out[:, 0, :groups].T


def _attention(u, slopes, lam, subln_g, *, out_scale, tq=512, tk_wide=2048):
    s = u.shape[0]
    dv = DA_V_DIM
    nh = DA_HEADS
    smem = pl.BlockSpec(memory_space=pltpu.SMEM)
    norms = _qk_block_norms(u, tq=tq)
    n_sub = tk_wide // tq
    return pl.pallas_call(
        functools.partial(_attn_kernel, tq=tq, out_scale=out_scale),
        grid=(nh, s // tk_wide),
        in_specs=[
            smem, smem, smem, smem,
            pl.BlockSpec((tk_wide, dv), lambda h, g: (g, h)),
            pl.BlockSpec((s, dv), lambda h, g: (0, nh + h)),
            pl.BlockSpec((s, dv), lambda h, g: (0, 2 * nh + h)),
            pl.BlockSpec((tk_wide, dv), lambda h, g: (g, 3 * nh + h)),
            pl.BlockSpec((1, dv), lambda h, g: (0, 0)),
        ],
        out_specs=pl.BlockSpec((tk_wide, dv), lambda h, g: (g, h)),
        out_shape=jax.ShapeDtypeStruct((s, nh * dv), BF16),
        scratch_shapes=[pltpu.VMEM((n_sub, 2, tq, dv), F32), pltpu.VMEM((n_sub, 2, tq, LANES), F32),
                        pltpu.VMEM((2, tq, tk_wide), F32)],
        compiler_params=_cparams("arbitrary", "arbitrary"),
        name="diff_attention",
    )(slopes, lam, norms[:2 * nh], norms[2 * nh:], u, u, u, u, subln_g)


def _hgrn_exponent_matrix():
    c = HG_CHUNK
    w = np.zeros((len(HG_LEVELS) + 1, c, c), np.float32)
    for r in range(c):
        w[0, r, :r + 1] = 1.0
        for li, m in enumerate(HG_LEVELS):
            mid = (r // (2 * m)) * 2 * m + m
            if r % (2 * m) >= m:
                w[li + 1, r, mid:r + 1] = 1.0
            else:
                w[li + 1, r, r + 1:mid] = 1.0
    return w.reshape(-1, c)


def _hgrn_kernel(q_ref, f_ref, i_ref, lb_ref, w_ref, o_ref, st_ref, st0_ref, *, tt):
    c = HG_CHUNK

    @pl.when(pl.program_id(1) == 0)
    def _():
        st_ref[...] = jnp.zeros_like(st_ref)

    lb = lb_ref[...]
    wall = w_ref[...]
    row = lax.broadcasted_iota(jnp.int32, (c, 1), 0)
    r2 = lax.broadcasted_iota(jnp.int32, (c, c), 0)
    c2 = lax.broadcasted_iota(jnp.int32, (c, c), 1)
    row8 = lax.broadcasted_iota(jnp.int32, (HG_LEVELS[-1], 1), 0)
    nt = (((1,), (1,)), ((), ()))
    tn = (((0,), (0,)), ((), ()))

    def chunk(ci, carry):
        r0 = pl.multiple_of(ci * c, c)
        q = _silu(q_ref[pl.ds(r0, c), :])
        v = i_ref[pl.ds(r0, c), :]
        fg = lb + (1.0 - lb) * jax.nn.sigmoid(f_ref[pl.ds(r0, c), :])
        kk = 1.0 - fg
        g = jnp.log(fg)
        g_hi = g.astype(BF16)
        g_lo = (g - g_hi.astype(F32)).astype(BF16)
        ex = (jnp.dot(wall, g_hi, preferred_element_type=F32)
              + jnp.dot(wall, g_lo, preferred_element_type=F32))
        b = ex[0:c]
        st = st_ref[...]
        vb = v.astype(BF16)

        o = lax.dot_general((q * jnp.exp(b)).astype(BF16), st.astype(BF16), nt,
                            preferred_element_type=F32)

        a = jnp.zeros((c, c), F32)
        for li, m in enumerate(HG_LEVELS):
            e = jnp.exp(ex[(li + 1) * c:(li + 2) * c])
            upper = (row % (2 * m)) >= m
            qt = jnp.where(upper, q * e, 0.0).astype(BF16)
            kt = jnp.where(upper, 0.0, kk * e).astype(BF16)
            al = lax.dot_general(qt, kt, nt, preferred_element_type=F32)
            if 2 * m != c:
                al = jnp.where((r2 // (2 * m)) == (c2 // (2 * m)), al, 0.0)
            a = a + al
        o = o + jnp.dot(a.astype(BF16), vb, preferred_element_type=F32)

        parts = []
        grp = HG_LEVELS[-1]
        for blk in range(c // grp):
            sl = slice(grp * blk, grp * (blk + 1))
            bb, kb, v8, q8 = b[sl], kk[sl], v[sl], q[sl]
            od = jnp.zeros((grp, HG_DIM), F32)
            for s in range(grp):
                e = jnp.exp(jnp.where(row8 >= s, bb - bb[s:s + 1], -jnp.inf))
                w = jnp.sum(e * q8 * kb[s:s + 1], axis=-1, keepdims=True)
                od = od + w * v8[s:s + 1]
            parts.append(od)
        o_ref[pl.ds(r0, c), :] = o + jnp.concatenate(parts, axis=0)

        b_last = b[c - 1:c]
        kd = (kk * jnp.exp(b_last - b)).astype(BF16)
        st_ref[...] = jnp.exp(b_last) * st + lax.dot_general(vb, kd, tn, preferred_element_type=F32)
        return carry

    n_chunks = tt // c
    dk = HG_DIM
    st0_ref[...] = st_ref[...]

    def side_by_side(fn):
        return jnp.concatenate([fn(n * c) for n in range(n_chunks)], axis=1)

    fg = side_by_side(lambda r0: lb + (1.0 - lb) * jax.nn.sigmoid(f_ref[r0:r0 + c, :]))
    g = jnp.log(fg)
    g_hi = g.astype(BF16)
    g_lo = (g - g_hi.astype(F32)).astype(BF16)
    tril = wall[0:c]
    b = (jnp.dot(tril, g_hi, preferred_element_type=F32)
         + jnp.dot(tril, g_lo, preferred_element_type=F32))
    safe = jnp.min(b[c - 1:c, :]) >= -HG_SAFE_EXP
    eb = jnp.exp(b)
    eb_last = eb[c - 1:c, :]
    qb = (side_by_side(lambda r0: _silu(q_ref[r0:r0 + c, :])) * eb).astype(BF16)
    ke = (1.0 - fg) * jnp.exp(-b)
    keb = ke.astype(BF16)
    kd = (ke * eb_last).astype(BF16)
    vbs = [i_ref[n * c:(n + 1) * c, :].astype(BF16) for n in range(n_chunks)]
    lanes_of = lambda x, n: x[:, n * dk:(n + 1) * dk]
    causal = r2 >= c2
    a = [jnp.where(causal, lax.dot_general(lanes_of(qb, n), lanes_of(keb, n), nt, preferred_element_type=F32),
                   0.0).astype(BF16) for n in range(n_chunks)]
    o_intra = [jnp.dot(a[n], vbs[n], preferred_element_type=F32) for n in range(n_chunks)]
    upd = [lax.dot_general(vbs[n], lanes_of(kd, n), tn, preferred_element_type=F32) for n in range(n_chunks)]
    st = st_ref[...]
    for n in range(n_chunks):
        o_ref[n * c:(n + 1) * c, :] = o_intra[n] + lax.dot_general(
            lanes_of(qb, n), st.astype(BF16), nt, preferred_element_type=F32)
        st = lanes_of(eb_last, n) * st + upd[n]
    st_ref[...] = st

    @pl.when(jnp.logical_not(safe))
    def _():
        st_ref[...] = st0_ref[...]
        lax.fori_loop(0, n_chunks, chunk, 0, unroll=2)


def _hgrn_core(u, lb, *, tt=2048):
    s = u.shape[0]
    nh, dk = HG_HEADS, HG_DIM
    wall = jnp.asarray(_hgrn_exponent_matrix(), BF16)
    return pl.pallas_call(
        functools.partial(_hgrn_kernel, tt=tt),
        grid=(nh, s // tt),
        in_specs=[
            pl.BlockSpec((tt, dk), lambda h, t: (t, h)),
            pl.BlockSpec((tt, dk), lambda h, t: (t, nh + h)),
            pl.BlockSpec((tt, dk), lambda h, t: (t, 2 * nh + h)),
            pl.BlockSpec((1, dk), lambda h, t: (0, h)),
            pl.BlockSpec(wall.shape, lambda h, t: (0, 0)),
        ],
        out_specs=pl.BlockSpec((tt, dk), lambda h, t: (t, h)),
        out_shape=jax.ShapeDtypeStruct((s, nh * dk), F32),
        scratch_shapes=[pltpu.VMEM((dk, dk), F32), pltpu.VMEM((dk, dk), F32)],
        compiler_params=_cparams("arbitrary", "arbitrary"),
        name="hgrn2_core",
    )(u, u, u, lb, wall)


def _conv_kernel(a_ref, ag_ref, gate_ref, dw_ref, dwb_ref, lng_ref, lnb_ref, o_ref, ybuf, cbuf, shbuf,
                 *, tt, cw):
    halo = CONV_HALO
    d = a_ref.shape[1]

    @pl.when(pl.program_id(0) == 0)
    def _():
        ybuf[0:halo, :] = jnp.zeros((halo, d), F32)

    @pl.when(pl.program_id(0) > 0)
    def _():
        ybuf[0:halo, :] = ybuf[tt:tt + halo, :]

    strip = 16

    def glu(si, carry):
        r = pl.multiple_of(si * strip, strip)
        ybuf[pl.ds(pl.multiple_of(halo + r, strip), strip), :] = (
            a_ref[pl.ds(r, strip), :] * _sigmoid(ag_ref[pl.ds(r, strip), :]))
        return carry

    lax.fori_loop(0, tt // strip, glu, 0)
    first = halo - (CONV_WIDTH - 1)

    def lanes(ci, carry):
        c0 = pl.multiple_of(ci * cw, cw)
        acc = jnp.broadcast_to(dwb_ref[:, pl.ds(c0, cw)], (tt, cw))
        for rho in range(SUBLANES):
            taps = [k for k in range(CONV_WIDTH) if (first + k) % SUBLANES == rho]
            if not taps:
                continue
            span = tt + SUBLANES * max((first + k) // SUBLANES for k in taps)
            if rho:
                shbuf[rho, 0:span, :] = ybuf[rho:rho + span, pl.ds(c0, cw)]
            for k in taps:
                a8 = SUBLANES * ((first + k) // SUBLANES)
                yk = shbuf[rho, a8:a8 + tt, :] if rho else ybuf[a8:a8 + tt, pl.ds(c0, cw)]
                acc = acc + yk * dw_ref[k:k + 1, pl.ds(c0, cw)]
        cbuf[:, pl.ds(c0, cw)] = acc
        return carry

    lax.fori_loop(0, d // cw, lanes, 0)

    def layernorm_gate(si, carry):
        r = pl.multiple_of(si * strip, strip)
        y = cbuf[pl.ds(r, strip), :]
        mu = jnp.mean(y, axis=-1, keepdims=True)
        yc = y - mu
        var = jnp.mean(yc * yc, axis=-1, keepdims=True)
        z = yc * lax.rsqrt(var + EPS) * lng_ref[...] + lnb_ref[...]
        o_ref[pl.ds(r, strip), :] = (_silu(z) * _silu(gate_ref[pl.ds(r, strip), :])).astype(o_ref.dtype)
        return carry

    lax.fori_loop(0, tt // strip, layernorm_gate, 0, unroll=4)


def _conv_core(u, dw, dw_b, ln_g, ln_b, *, tt=256, cw=LANES):
    s = u.shape[0]
    d = dw.shape[1]
    row = pl.BlockSpec((1, d), lambda t: (0, 0))
    return pl.pallas_call(
        functools.partial(_conv_kernel, tt=tt, cw=cw),
        grid=(s // tt,),
        in_specs=[
            pl.BlockSpec((tt, d), lambda t: (t, 0)),
            pl.BlockSpec((tt, d), lambda t: (t, 1)),
            pl.BlockSpec((tt, d), lambda t: (t, 2)),
            pl.BlockSpec(dw.shape, lambda t: (0, 0)),
            row, row, row,
        ],
        out_specs=pl.BlockSpec((tt, d), lambda t: (t, 0)),
        out_shape=jax.ShapeDtypeStruct((s, d), BF16),
        scratch_shapes=[pltpu.VMEM((tt + CONV_HALO, d), F32), pltpu.VMEM((tt, d), F32),
                        pltpu.VMEM((SUBLANES, tt + CONV_HALO, cw), F32)],
        compiler_params=_cparams("arbitrary"),
        name="conv_module_core",
    )(u, u, u, dw, dw_b, ln_g, ln_b)


def kernel(x, c, norm_g, ada_w, ada_b, attn_w_in, attn_w_out, attn_lam_q1, attn_lam_k1, attn_lam_q2, attn_lam_k2, attn_subln_g, hgrn_w_in, hgrn_w_out, hgrn_lb_logits, hgrn_gnorm_g, conv_w_in, conv_b_in, conv_dw, conv_dw_b, conv_ln_g, conv_ln_b, conv_w_out, conv_b_out, final_g):
    bsz, s, d = x.shape
    depth = norm_g.shape[0]
    assert bsz == 1 and c.shape == (1, d)
    xs = x.reshape(s, d)
    mods = _mods(c, ada_w, ada_b)

    slopes = (2.0 ** (-8.0 * jnp.arange(1, DA_HEADS + 1, dtype=F32) / DA_HEADS)) * LOG2E
    qk_cols = 2 * DA_HEADS * DA_QK_DIM
    attn_colscale = jnp.concatenate(
        [jnp.full((1, qk_cols), DA_QK_DIM ** -0.5 * LOG2E, F32), jnp.ones((1, 3 * qk_cols), F32)], axis=1)
    lb_all = jax.nn.softmax(hgrn_lb_logits.astype(F32), axis=0)
    lb_all = jnp.cumsum(lb_all, axis=0) - lb_all[0]

    def mod_rows(i):
        return tuple(mods[i, :, k * d:(k + 1) * d] for k in range(3))

    shift, scale, gate = mod_rows(0)
    h = _norm(xs, norm_g[0:1], scale, shift)
    for i in range(depth):
        kind, j = i % N_MIXERS, i // N_MIXERS
        bias_out = pre = None
        if kind == 0:
            u = _matmul(h, attn_w_in, j, out_dtype=BF16, colscale=attn_colscale)
            lam_init = 0.8 - 0.6 * math.exp(-0.3 * i)
            lam = (jnp.exp(jnp.sum(attn_lam_q1[j] * attn_lam_k1[j]))
                   - jnp.exp(jnp.sum(attn_lam_q2[j] * attn_lam_k2[j])) + lam_init).reshape(1)
            y = _attention(u, slopes, lam, attn_subln_g[j:j + 1], out_scale=1.0 - lam_init)
            w_out = attn_w_out[j]
        elif kind == 1:
            u = _matmul(h, hgrn_w_in, j, out_dtype=F32)
            y = _hgrn_core(u, lb_all[i:i + 1])
            pre = (u, u.shape[1] // d - 1, hgrn_gnorm_g[j:j + 1])
            w_out = hgrn_w_out[j]
        else:
            u = _matmul(h, conv_w_in, j, out_dtype=F32, bias=conv_b_in[j:j + 1])
            y = _conv_core(u, conv_dw[j], conv_dw_b[j:j + 1], conv_ln_g[j:j + 1], conv_ln_b[j:j + 1])
            w_out = conv_w_out[j]
            bias_out = conv_b_out[j:j + 1]
        if i + 1 < depth:
            shift_n, scale_n, gate_n = mod_rows(i + 1)
            xs, h = _outproj(y, w_out.astype(BF16), xs, gate, bias_out, norm_g[i + 1:i + 2], scale_n, shift_n,
                             modulate=True, write_x=True, out_dtype=BF16, pre=pre)
            gate = gate_n
        else:
            zero = jnp.zeros((1, d), F32)
            out, = _outproj(y, w_out.astype(BF16), xs, gate, bias_out, final_g.reshape(1, d), zero, zero,
                            modulate=False, write_x=False, out_dtype=F32, pre=pre)
    return out.reshape(bsz, s, d)
```

```python
import functools
import math

import numpy as np
import jax
import jax.numpy as jnp
from jax import lax
from jax.experimental import pallas as pl
from jax.experimental.pallas import tpu as pltpu

F32 = jnp.float32
BF16 = jnp.bfloat16

EPS = 1e-6
LOG2E = 1.4426950408889634
N_MIXERS = 3

DA_HEADS = 8
DA_QK_DIM = 128
DA_V_DIM = 256
ATT_SKIP_LOG2 = 160.0
ATT_NORM_MARGIN = 1.01
HG_HEADS = 16
HG_DIM = 128
HG_CHUNK = 64
HG_LEVELS = (32, 16, 8)
HG_SAFE_EXP = 80.0
CONV_WIDTH = 31
CONV_HALO = 32

SUBLANES = 8
LANES = 128
VMEM_LIMIT_BYTES = 56 * 1024 * 1024


def _cparams(*sem):
    return pltpu.CompilerParams(dimension_semantics=sem, vmem_limit_bytes=VMEM_LIMIT_BYTES)


def _sigmoid(x):
    return 0.5 * jnp.tanh(0.5 * x) + 0.5


def _silu(x):
    h = 0.5 * x
    return h + h * jnp.tanh(h)


def _mods_kernel(c_ref, w_ref, b_ref, o_ref, ca_ref, acc_ref, *, kc, ln):
    kstep = pl.program_id(1)
    kb, n = w_ref.shape[1], w_ref.shape[2]
    ca_ref[...] = _silu(c_ref[...])

    @pl.when(kstep == 0)
    def _():
        acc_ref[...] = jnp.zeros_like(acc_ref)

    for lb in range(n // ln):
        cols = slice(lb * ln, (lb + 1) * ln)

        def body(i, acc):
            k0 = pl.multiple_of(i * kc, kc)
            prod = w_ref[0, pl.ds(k0, kc), cols] * ca_ref[pl.ds(k0, kc), :]
            return acc + prod.reshape(kc // SUBLANES, SUBLANES, ln).sum(axis=0)

        acc_ref[:, cols] = lax.fori_loop(0, kb // kc, body, acc_ref[:, cols], unroll=4)

    @pl.when(kstep == pl.num_programs(1) - 1)
    def _():
        o_ref[0] = jnp.sum(acc_ref[...], axis=0, keepdims=True) + b_ref[0]


def _mods(c, ada_w, ada_b, *, kb=512, kc=32, ln=1024):
    depth, d, n = ada_w.shape
    return pl.pallas_call(
        functools.partial(_mods_kernel, kc=kc, ln=ln),
        grid=(depth, d // kb),
        in_specs=[
            pl.BlockSpec((kb, 1), lambda i, k: (k, 0)),
            pl.BlockSpec((1, kb, n), lambda i, k: (i, k, 0)),
            pl.BlockSpec((1, 1, n), lambda i, k: (i, 0, 0)),
        ],
        out_specs=pl.BlockSpec((1, 1, n), lambda i, k: (i, 0, 0)),
        out_shape=jax.ShapeDtypeStruct((depth, 1, n), F32),
        scratch_shapes=[pltpu.VMEM((kb, 1), F32), pltpu.VMEM((SUBLANES, n), F32)],
        compiler_params=_cparams("arbitrary", "arbitrary"),
        name="adaln_mods",
    )(c.reshape(d, 1), ada_w, ada_b.reshape(depth, 1, n))


def _norm_kernel(x_ref, g_ref, sc_ref, sh_ref, o_ref):
    x = x_ref[...]
    y = x * lax.rsqrt(jnp.mean(x * x, axis=-1, keepdims=True) + EPS) * g_ref[...]
    o_ref[...] = (y * (1.0 + sc_ref[...]) + sh_ref[...]).astype(o_ref.dtype)


def _norm(x, gain, scale, shift, *, tm=512):
    s, d = x.shape
    row = pl.BlockSpec((1, d), lambda i: (0, 0))
    return pl.pallas_call(
        _norm_kernel,
        grid=(s // tm,),
        in_specs=[pl.BlockSpec((tm, d), lambda i: (i, 0)), row, row, row],
        out_specs=pl.BlockSpec((tm, d), lambda i: (i, 0)),
        out_shape=jax.ShapeDtypeStruct((s, d), BF16),
        compiler_params=_cparams("arbitrary"),
        name="rmsnorm_mod",
    )(x, gain, scale, shift)


def _mm_kernel(*refs, has_bias, has_colscale):
    a_ref, w_ref = refs[0], refs[1]
    o_ref, wb_ref = refs[-2], refs[-1]
    pos = 2

    @pl.when(pl.program_id(1) == 0)
    def _():
        wb_ref[...] = w_ref[...].astype(BF16)

    acc = jnp.dot(a_ref[...], wb_ref[...], preferred_element_type=F32)
    if has_bias:
        acc = acc + refs[pos][...]
        pos += 1
    if has_colscale:
        acc = acc * refs[pos][...]
    o_ref[...] = acc.astype(o_ref.dtype)


def _matmul(a, w, layer, *, out_dtype, bias=None, colscale=None, tm=1024, tn=1024):
    m, k = a.shape
    n_out = w.shape[2]
    in_specs = [
        pl.BlockSpec((tm, k), lambda j, i: (i, 0)),
        pl.BlockSpec((None, k, tn), lambda j, i: (layer, 0, j)),
    ]
    args = [a, w]
    rowspec = pl.BlockSpec((1, tn), lambda j, i: (0, j))
    for extra in (bias, colscale):
        if extra is not None:
            in_specs.append(rowspec)
            args.append(extra)
    return pl.pallas_call(
        functools.partial(_mm_kernel, has_bias=bias is not None, has_colscale=colscale is not None),
        grid=(n_out // tn, m // tm),
        in_specs=in_specs,
        out_specs=pl.BlockSpec((tm, tn), lambda j, i: (i, j)),
        out_shape=jax.ShapeDtypeStruct((m, n_out), out_dtype),
        scratch_shapes=[pltpu.VMEM((k, tn), BF16)],
        compiler_params=_cparams("arbitrary", "arbitrary"),
        name="matmul_in",
    )(*args)


def _outproj_kernel(*refs, has_bias, has_pre, modulate, write_x):
    a_ref, w_ref, x_ref, gate_ref = refs[:4]
    pos = 4
    a = a_ref[...]
    if has_pre:
        a = a * lax.rsqrt(jnp.mean(a * a, axis=-1, keepdims=True) + EPS) * refs[pos + 1][...]
        a = (a * _silu(refs[pos][...])).astype(BF16)
        pos += 2
    y = jnp.dot(a, w_ref[...], preferred_element_type=F32)
    if has_bias:
        y = y + refs[pos][...]
        pos += 1
    g_ref, sc_ref, sh_ref = refs[pos:pos + 3]
    xn = x_ref[...] + gate_ref[...] * y
    if write_x:
        refs[-2][...] = xn
    hn = xn * lax.rsqrt(jnp.mean(xn * xn, axis=-1, keepdims=True) + EPS) * g_ref[...]
    if modulate:
        hn = hn * (1.0 + sc_ref[...]) + sh_ref[...]
    refs[-1][...] = hn.astype(refs[-1].dtype)


def _outproj(a, w, x, gate, bias, gain, scale, shift, *, modulate, write_x, out_dtype, pre=None, tm=None):
    m, k = a.shape
    d = w.shape[1]
    if tm is None:
        tm = 512
    row = pl.BlockSpec((1, d), lambda i: (0, 0))
    tile = pl.BlockSpec((tm, d), lambda i: (i, 0))
    w_spec = pl.BlockSpec((k, d), lambda i: (0, 0), pipeline_mode=pl.Buffered(1))
    in_specs = [pl.BlockSpec((tm, k), lambda i: (i, 0)), w_spec, tile, row]
    args = [a, w, x, gate]
    if pre is not None:
        u, col_block, pre_gain = pre
        in_specs += [pl.BlockSpec((tm, k), lambda i: (i, col_block)), pl.BlockSpec((1, k), lambda i: (0, 0))]
        args += [u, pre_gain]
    if bias is not None:
        in_specs.append(row)
        args.append(bias)
    in_specs += [row, row, row]
    args += [gain, scale, shift]
    out_specs = [tile]
    out_shape = [jax.ShapeDtypeStruct((m, d), out_dtype)]
    if write_x:
        out_specs = [tile, tile]
        out_shape = [jax.ShapeDtypeStruct((m, d), F32)] + out_shape
    return pl.pallas_call(
        functools.partial(_outproj_kernel, has_bias=bias is not None, has_pre=pre is not None,
                          modulate=modulate, write_x=write_x),
        grid=(m // tm,),
        in_specs=in_specs,
        out_specs=out_specs,
        out_shape=out_shape,
        compiler_params=_cparams("arbitrary"),
        name="matmul_out",
    )(*args)


def _attn_kernel(slope_ref, lam_ref, qn_ref, kn_ref, q_ref, k_ref, v_ref, g_ref, sg_ref, o_ref,
                 acc_ref, l_ref, s_ref, *, tq, out_scale):
    h = pl.program_id(0)
    dk = DA_QK_DIM
    slope = slope_ref[h]
    lam = lam_ref[0]
    lanes = LANES

    def scores(mi, k0, tk, m_old, diag_col):
        colf = lax.broadcasted_iota(jnp.int32, (1, tk), 1).astype(F32)
        bias = slope * (colf + (k0 - q0).astype(F32))
        qm = q_ref[rows, mi * dk:(mi + 1) * dk]
        km = k_ref[pl.ds(k0, tk), mi * dk:(mi + 1) * dk]
        s = lax.dot_general(qm, km, (((1,), (1,)), ((), ())), preferred_element_type=F32) + bias
        if diag_col is not None:
            r = lax.broadcasted_iota(jnp.int32, (tq, tk), 0)
            c = lax.broadcasted_iota(jnp.int32, (tq, tk), 1)
            s = jnp.where(r + diag_col >= c, s, -jnp.inf)
        s_ref[mi, :, 0:tk] = s
        smax = jnp.max(s, axis=-1, keepdims=True)
        return smax if m_old is None else jnp.maximum(m_old, smax)

    def accumulate(mi, k0, tk, m_old, m_new):
        vb = v_ref[pl.ds(k0, tk), :]
        p = jnp.exp2(s_ref[mi, :, 0:tk] - m_new)
        psum = p[:, 0:lanes]
        for t in range(1, tk // lanes):
            psum = psum + p[:, t * lanes:(t + 1) * lanes]
        if m_old is None:
            l_ref[sub, mi] = psum
            acc_ref[sub, mi] = jnp.dot(p.astype(BF16), vb, preferred_element_type=F32)
        else:
            alpha = jnp.exp2(m_old - m_new)
            l_ref[sub, mi] = alpha * l_ref[sub, mi] + psum
            acc_ref[sub, mi] = alpha * acc_ref[sub, mi] + jnp.dot(p.astype(BF16), vb,
                                                                    preferred_element_type=F32)

    def tile(k0, tk, m, diag_col=None):
        k0 = pl.multiple_of(k0, tq)
        old = (None, None) if m is None else m
        new = tuple(scores(mi, k0, tk, old[mi], diag_col) for mi in range(2))
        for mi in range(2):
            accumulate(mi, k0, tk, old[mi], new[mi])
        return new

    def dead(k0, tk):
        b0 = k0 // tq
        top_bias = slope * (k0 + (tk - 1) - q0).astype(F32)
        ok = None
        for mi in range(2):
            kmax = kn_ref[2 * h + mi, b0]
            for t in range(1, tk // tq):
                kmax = jnp.maximum(kmax, kn_ref[2 * h + mi, b0 + t])
            below = qn[mi] * kmax + top_bias < floor[mi]
            ok = below if ok is None else jnp.logical_and(ok, below)
        return ok

    def visit(k0, w, m):
        half = w // 2
        near_only = functools.partial(lambda m, k0, half: tile(k0 + half, half, m), k0=k0, half=half)
        whole = functools.partial(lambda m, k0, w: tile(k0, w, m), k0=k0, w=w)
        return lax.cond(dead(k0, w), lambda m: m, lambda m: lax.cond(dead(k0, half), near_only, whole, m), m)

    wide = s_ref.shape[2]
    n_sub = wide // tq
    for sub in range(n_sub):
        i = n_sub * pl.program_id(1) + sub
        q0 = i * tq
        rows = slice(sub * tq, (sub + 1) * tq)
        odd = sub % 2
        m = tile(q0 - tq, 2 * tq, None, diag_col=tq) if odd else tile(q0, tq, None, diag_col=0)
        qn = [qn_ref[2 * h + mi, i] * ATT_NORM_MARGIN for mi in range(2)]
        floor = [jnp.min(m[mi]) - ATT_SKIP_LOG2 for mi in range(2)]
        w = 2 * tq
        end = q0 - odd * tq
        while w < wide:
            if (sub // (w // tq)) % 2:
                m = visit(end - w, w, m)
                end = end - w
            w *= 2
        lax.fori_loop(0, pl.program_id(1), lambda jj, m: visit(jj * wide, wide, m), m)

        inv1 = 1.0 / jnp.sum(l_ref[sub, 0], axis=-1, keepdims=True)
        inv2 = lam / jnp.sum(l_ref[sub, 1], axis=-1, keepdims=True)
        o = acc_ref[sub, 0] * inv1 - acc_ref[sub, 1] * inv2
        y = o * lax.rsqrt(jnp.mean(o * o, axis=-1, keepdims=True) + EPS) * sg_ref[...] * out_scale
        o_ref[rows, :] = (y * _silu(g_ref[rows, :].astype(F32))).astype(o_ref.dtype)


def _block_norm_kernel(x_ref, g_ref, o_ref):
    x = x_ref[...]
    rs = jnp.dot(x * x, g_ref[...], preferred_element_type=F32)
    o_ref[0] = jnp.sqrt(jnp.max(rs, axis=0, keepdims=True))


def _qk_block_norms(u, *, tq):
    s = u.shape[0]
    groups = 4 * DA_HEADS
    cols = groups * DA_QK_DIM
    assert groups <= LANES
    ind = np.zeros((cols, LANES), np.float32)
    ind[np.arange(cols), np.arange(cols) // DA_QK_DIM] = 1.0
    out = pl.pallas_call(
        _block_norm_kernel,
        grid=(s // tq,),
        in_specs=[pl.BlockSpec((tq, cols), lambda i: (i, 0)), pl.BlockSpec((cols, LANES), lambda i: (0, 0))],
        out_specs=pl.BlockSpec((1, 1, LANES), lambda i: (i, 0, 0)),
        out_shape=jax.ShapeDtypeStruct((s // tq, 1, LANES), F32),
        compiler_params=_cparams("arbitrary"),
        name="qk_block_norms",
    )(u, jnp.asarray(ind, BF16))
    return out[:, 0, :groups].T


def _attention(u, slopes, lam, subln_g, *, out_scale, tq=512, tk_wide=2048):
    s = u.shape[0]
    dv = DA_V_DIM
    nh = DA_HEADS
    smem = pl.BlockSpec(memory_space=pltpu.SMEM)
    norms = _qk_block_norms(u, tq=tq)
    n_sub = tk_wide // tq
    return pl.pallas_call(
        functools.partial(_attn_kernel, tq=tq, out_scale=out_scale),
        grid=(nh, s // tk_wide),
        in_specs=[
            smem, smem, smem, smem,
            pl.BlockSpec((tk_wide, dv), lambda h, g: (g, h)),
            pl.BlockSpec((s, dv), lambda h, g: (0, nh + h)),
            pl.BlockSpec((s, dv), lambda h, g: (0, 2 * nh + h)),
            pl.BlockSpec((tk_wide, dv), lambda h, g: (g, 3 * nh + h)),
            pl.BlockSpec((1, dv), lambda h, g: (0, 0)),
        ],
        out_specs=pl.BlockSpec((tk_wide, dv), lambda h, g: (g, h)),
        out_shape=jax.ShapeDtypeStruct((s, nh * dv), BF16),
        scratch_shapes=[pltpu.VMEM((n_sub, 2, tq, dv), F32), pltpu.VMEM((n_sub, 2, tq, LANES), F32),
                        pltpu.VMEM((2, tq, tk_wide), F32)],
        compiler_params=_cparams("arbitrary", "arbitrary"),
        name="diff_attention",
    )(slopes, lam, norms[:2 * nh], norms[2 * nh:], u, u, u, u, subln_g)


def _hgrn_exponent_matrix():
    c = HG_CHUNK
    w = np.zeros((len(HG_LEVELS) + 1, c, c), np.float32)
    for r in range(c):
        w[0, r, :r + 1] = 1.0
        for li, m in enumerate(HG_LEVELS):
            mid = (r // (2 * m)) * 2 * m + m
            if r % (2 * m) >= m:
                w[li + 1, r, mid:r + 1] = 1.0
            else:
                w[li + 1, r, r + 1:mid] = 1.0
    return w.reshape(-1, c)


def _hgrn_kernel(q_ref, f_ref, i_ref, lb_ref, w_ref, o_ref, st_ref, st0_ref, *, tt):
    c = HG_CHUNK

    @pl.when(pl.program_id(1) == 0)
    def _():
        st_ref[...] = jnp.zeros_like(st_ref)

    lb = lb_ref[...]
    wall = w_ref[...]
    row = lax.broadcasted_iota(jnp.int32, (c, 1), 0)
    r2 = lax.broadcasted_iota(jnp.int32, (c, c), 0)
    c2 = lax.broadcasted_iota(jnp.int32, (c, c), 1)
    row8 = lax.broadcasted_iota(jnp.int32, (HG_LEVELS[-1], 1), 0)
    nt = (((1,), (1,)), ((), ()))
    tn = (((0,), (0,)), ((), ()))

    def chunk(ci, carry):
        r0 = pl.multiple_of(ci * c, c)
        q = _silu(q_ref[pl.ds(r0, c), :])
        v = i_ref[pl.ds(r0, c), :]
        fg = lb + (1.0 - lb) * jax.nn.sigmoid(f_ref[pl.ds(r0, c), :])
        kk = 1.0 - fg
        g = jnp.log(fg)
        g_hi = g.astype(BF16)
        g_lo = (g - g_hi.astype(F32)).astype(BF16)
        ex = (jnp.dot(wall, g_hi, preferred_element_type=F32)
              + jnp.dot(wall, g_lo, preferred_element_type=F32))
        b = ex[0:c]
        st = st_ref[...]
        vb = v.astype(BF16)

        o = lax.dot_general((q * jnp.exp(b)).astype(BF16), st.astype(BF16), nt,
                            preferred_element_type=F32)

        a = jnp.zeros((c, c), F32)
        for li, m in enumerate(HG_LEVELS):
            e = jnp.exp(ex[(li + 1) * c:(li + 2) * c])
            upper = (row % (2 * m)) >= m
            qt = jnp.where(upper, q * e, 0.0).astype(BF16)
            kt = jnp.where(upper, 0.0, kk * e).astype(BF16)
            al = lax.dot_general(qt, kt, nt, preferred_element_type=F32)
            if 2 * m != c:
                al = jnp.where((r2 // (2 * m)) == (c2 // (2 * m)), al, 0.0)
            a = a + al
        o = o + jnp.dot(a.astype(BF16), vb, preferred_element_type=F32)

        parts = []
        grp = HG_LEVELS[-1]
        for blk in range(c // grp):
            sl = slice(grp * blk, grp * (blk + 1))
            bb, kb, v8, q8 = b[sl], kk[sl], v[sl], q[sl]
            od = jnp.zeros((grp, HG_DIM), F32)
            for s in range(grp):
                e = jnp.exp(jnp.where(row8 >= s, bb - bb[s:s + 1], -jnp.inf))
                w = jnp.sum(e * q8 * kb[s:s + 1], axis=-1, keepdims=True)
                od = od + w * v8[s:s + 1]
            parts.append(od)
        o_ref[pl.ds(r0, c), :] = o + jnp.concatenate(parts, axis=0)

        b_last = b[c - 1:c]
        kd = (kk * jnp.exp(b_last - b)).astype(BF16)
        st_ref[...] = jnp.exp(b_last) * st + lax.dot_general(vb, kd, tn, preferred_element_type=F32)
        return carry

    n_chunks = tt // c
    dk = HG_DIM
    st0_ref[...] = st_ref[...]

    def side_by_side(fn):
        return jnp.concatenate([fn(n * c) for n in range(n_chunks)], axis=1)

    fg = side_by_side(lambda r0: lb + (1.0 - lb) * jax.nn.sigmoid(f_ref[r0:r0 + c, :]))
    g = jnp.log(fg)
    g_hi = g.astype(BF16)
    g_lo = (g - g_hi.astype(F32)).astype(BF16)
    tril = wall[0:c]
    b = (jnp.dot(tril, g_hi, preferred_element_type=F32)
         + jnp.dot(tril, g_lo, preferred_element_type=F32))
    safe = jnp.min(b[c - 1:c, :]) >= -HG_SAFE_EXP
    eb = jnp.exp(b)
    eb_last = eb[c - 1:c, :]
    qb = (side_by_side(lambda r0: _silu(q_ref[r0:r0 + c, :])) * eb).astype(BF16)
    ke = (1.0 - fg) * jnp.exp(-b)
    keb = ke.astype(BF16)
    kd = (ke * eb_last).astype(BF16)
    vbs = [i_ref[n * c:(n + 1) * c, :].astype(BF16) for n in range(n_chunks)]
    lanes_of = lambda x, n: x[:, n * dk:(n + 1) * dk]
    causal = r2 >= c2
    a = [jnp.where(causal, lax.dot_general(lanes_of(qb, n), lanes_of(keb, n), nt, preferred_element_type=F32),
                   0.0).astype(BF16) for n in range(n_chunks)]
    o_intra = [jnp.dot(a[n], vbs[n], preferred_element_type=F32) for n in range(n_chunks)]
    upd = [lax.dot_general(vbs[n], lanes_of(kd, n), tn, preferred_element_type=F32) for n in range(n_chunks)]
    st = st_ref[...]
    for n in range(n_chunks):
        o_ref[n * c:(n + 1) * c, :] = o_intra[n] + lax.dot_general(
            lanes_of(qb, n), st.astype(BF16), nt, preferred_element_type=F32)
        st = lanes_of(eb_last, n) * st + upd[n]
    st_ref[...] = st

    @pl.when(jnp.logical_not(safe))
    def _():
        st_ref[...] = st0_ref[...]
        lax.fori_loop(0, n_chunks, chunk, 0, unroll=2)


def _hgrn_core(u, lb, *, tt=2048):
    s = u.shape[0]
    nh, dk = HG_HEADS, HG_DIM
    wall = jnp.asarray(_hgrn_exponent_matrix(), BF16)
    return pl.pallas_call(
        functools.partial(_hgrn_kernel, tt=tt),
        grid=(nh, s // tt),
        in_specs=[
            pl.BlockSpec((tt, dk), lambda h, t: (t, h)),
            pl.BlockSpec((tt, dk), lambda h, t: (t, nh + h)),
            pl.BlockSpec((tt, dk), lambda h, t: (t, 2 * nh + h)),
            pl.BlockSpec((1, dk), lambda h, t: (0, h)),
            pl.BlockSpec(wall.shape, lambda h, t: (0, 0)),
        ],
        out_specs=pl.BlockSpec((tt, dk), lambda h, t: (t, h)),
        out_shape=jax.ShapeDtypeStruct((s, nh * dk), F32),
        scratch_shapes=[pltpu.VMEM((dk, dk), F32), pltpu.VMEM((dk, dk), F32)],
        compiler_params=_cparams("arbitrary", "arbitrary"),
        name="hgrn2_core",
    )(u, u, u, lb, wall)


def _conv_kernel(a_ref, ag_ref, gate_ref, dw_ref, dwb_ref, lng_ref, lnb_ref, o_ref, ybuf, cbuf, shbuf,
                 *, tt, cw):
    halo = CONV_HALO
    d = a_ref.shape[1]

    @pl.when(pl.program_id(0) == 0)
    def _():
        ybuf[0:halo, :] = jnp.zeros((halo, d), F32)

    @pl.when(pl.program_id(0) > 0)
    def _():
        ybuf[0:halo, :] = ybuf[tt:tt + halo, :]

    strip = 16

    def glu(si, carry):
        r = pl.multiple_of(si * strip, strip)
        ybuf[pl.ds(pl.multiple_of(halo + r, strip), strip), :] = (
            a_ref[pl.ds(r, strip), :] * _sigmoid(ag_ref[pl.ds(r, strip), :]))
        return carry

    lax.fori_loop(0, tt // strip, glu, 0)
    first = halo - (CONV_WIDTH - 1)

    def lanes(ci, carry):
        c0 = pl.multiple_of(ci * cw, cw)
        acc = jnp.broadcast_to(dwb_ref[:, pl.ds(c0, cw)], (tt, cw))
        for rho in range(SUBLANES):
            taps = [k for k in range(CONV_WIDTH) if (first + k) % SUBLANES == rho]
            if not taps:
                continue
            span = tt + SUBLANES * max((first + k) // SUBLANES for k in taps)
            if rho:
                shbuf[rho, 0:span, :] = ybuf[rho:rho + span, pl.ds(c0, cw)]
            for k in taps:
                a8 = SUBLANES * ((first + k) // SUBLANES)
                yk = shbuf[rho, a8:a8 + tt, :] if rho else ybuf[a8:a8 + tt, pl.ds(c0, cw)]
                acc = acc + yk * dw_ref[k:k + 1, pl.ds(c0, cw)]
        cbuf[:, pl.ds(c0, cw)] = acc
        return carry

    lax.fori_loop(0, d // cw, lanes, 0)

    def layernorm_gate(si, carry):
        r = pl.multiple_of(si * strip, strip)
        y = cbuf[pl.ds(r, strip), :]
        mu = jnp.mean(y, axis=-1, keepdims=True)
        yc = y - mu
        var = jnp.mean(yc * yc, axis=-1, keepdims=True)
        z = yc * lax.rsqrt(var + EPS) * lng_ref[...] + lnb_ref[...]
        o_ref[pl.ds(r, strip), :] = (_silu(z) * _silu(gate_ref[pl.ds(r, strip), :])).astype(o_ref.dtype)
        return carry

    lax.fori_loop(0, tt // strip, layernorm_gate, 0, unroll=4)


def _conv_core(u, dw, dw_b, ln_g, ln_b, *, tt=256, cw=LANES):
    s = u.shape[0]
    d = dw.shape[1]
    row = pl.BlockSpec((1, d), lambda t: (0, 0))
    return pl.pallas_call(
        functools.partial(_conv_kernel, tt=tt, cw=cw),
        grid=(s // tt,),
        in_specs=[
            pl.BlockSpec((tt, d), lambda t: (t, 0)),
            pl.BlockSpec((tt, d), lambda t: (t, 1)),
            pl.BlockSpec((tt, d), lambda t: (t, 2)),
            pl.BlockSpec(dw.shape, lambda t: (0, 0)),
            row, row, row,
        ],
        out_specs=pl.BlockSpec((tt, d), lambda t: (t, 0)),
        out_shape=jax.ShapeDtypeStruct((s, d), BF16),
        scratch_shapes=[pltpu.VMEM((tt + CONV_HALO, d), F32), pltpu.VMEM((tt, d), F32),
                        pltpu.VMEM((SUBLANES, tt + CONV_HALO, cw), F32)],
        compiler_params=_cparams("arbitrary"),
        name="conv_module_core",
    )(u, u, u, dw, dw_b, ln_g, ln_b)


def kernel(x, c, norm_g, ada_w, ada_b, attn_w_in, attn_w_out, attn_lam_q1, attn_lam_k1, attn_lam_q2, attn_lam_k2, attn_subln_g, hgrn_w_in, hgrn_w_out, hgrn_lb_logits, hgrn_gnorm_g, conv_w_in, conv_b_in, conv_dw, conv_dw_b, conv_ln_g, conv_ln_b, conv_w_out, conv_b_out, final_g):
    bsz, s, d = x.shape
    depth = norm_g.shape[0]
    assert bsz == 1 and c.shape == (1, d)
    xs = x.reshape(s, d)
    mods = _mods(c, ada_w, ada_b)

    slopes = (2.0 ** (-8.0 * jnp.arange(1, DA_HEADS + 1, dtype=F32) / DA_HEADS)) * LOG2E
    qk_cols = 2 * DA_HEADS * DA_QK_DIM
    attn_colscale = jnp.concatenate(
        [jnp.full((1, qk_cols), DA_QK_DIM ** -0.5 * LOG2E, F32), jnp.ones((1, 3 * qk_cols), F32)], axis=1)
    lb_all = jax.nn.softmax(hgrn_lb_logits.astype(F32), axis=0)
    lb_all = jnp.cumsum(lb_all, axis=0) - lb_all[0]

    def mod_rows(i):
        return tuple(mods[i, :, k * d:(k + 1) * d] for k in range(3))

    shift, scale, gate = mod_rows(0)
    h = _norm(xs, norm_g[0:1], scale, shift)
    for i in range(depth):
        kind, j = i % N_MIXERS, i // N_MIXERS
        bias_out = pre = None
        if kind == 0:
            u = _matmul(h, attn_w_in, j, out_dtype=BF16, colscale=attn_colscale)
            lam_init = 0.8 - 0.6 * math.exp(-0.3 * i)
            lam = (jnp.exp(jnp.sum(attn_lam_q1[j] * attn_lam_k1[j]))
                   - jnp.exp(jnp.sum(attn_lam_q2[j] * attn_lam_k2[j])) + lam_init).reshape(1)
            y = _attention(u, slopes, lam, attn_subln_g[j:j + 1], out_scale=1.0 - lam_init)
            w_out = attn_w_out[j]
        elif kind == 1:
            u = _matmul(h, hgrn_w_in, j, out_dtype=F32)
            y = _hgrn_core(u, lb_all[i:i + 1])
            pre = (u, u.shape[1] // d - 1, hgrn_gnorm_g[j:j + 1])
            w_out = hgrn_w_out[j]
        else:
            u = _matmul(h, conv_w_in, j, out_dtype=F32, bias=conv_b_in[j:j + 1])
            y = _conv_core(u, conv_dw[j], conv_dw_b[j:j + 1], conv_ln_g[j:j + 1], conv_ln_b[j:j + 1])
            w_out = conv_w_out[j]
            bias_out = conv_b_out[j:j + 1]
        if i + 1 < depth:
            shift_n, scale_n, gate_n = mod_rows(i + 1)
            xs, h = _outproj(y, w_out.astype(BF16), xs, gate, bias_out, norm_g[i + 1:i + 2], scale_n, shift_n,
                             modulate=True, write_x=True, out_dtype=BF16, pre=pre)
            gate = gate_n
        else:
            zero = jnp.zeros((1, d), F32)
            out, = _outproj(y, w_out.astype(BF16), xs, gate, bias_out, final_g.reshape(1, d), zero, zero,
                            modulate=False, write_x=False, out_dtype=F32, pre=pre)
    return out.reshape(bsz, s, d)
```

```python
import functools
import math

import numpy as np
import jax
import jax.numpy as jnp
from jax import lax
from jax.experimental import pallas as pl
from jax.experimental.pallas import tpu as pltpu

F32 = jnp.float32
BF16 = jnp.bfloat16

EPS = 1e-6
LOG2E = 1.4426950408889634
N_MIXERS = 3

DA_HEADS = 8
DA_QK_DIM = 128
DA_V_DIM = 256
ATT_SKIP_LOG2 = 160.0
ATT_NORM_MARGIN = 1.01
HG_HEADS = 16
HG_DIM = 128
HG_CHUNK = 64
HG_LEVELS = (32, 16, 8)
HG_SAFE_EXP = 80.0
CONV_WIDTH = 31
CONV_HALO = 32

SUBLANES = 8
LANES = 128
VMEM_LIMIT_BYTES = 56 * 1024 * 1024


def _cparams(*sem):
    return pltpu.CompilerParams(dimension_semantics=sem, vmem_limit_bytes=VMEM_LIMIT_BYTES)


def _sigmoid(x):
    return 0.5 * jnp.tanh(0.5 * x) + 0.5


def _silu(x):
    h = 0.5 * x
    return h + h * jnp.tanh(h)


def _mods_kernel(c_ref, w_ref, b_ref, o_ref, ca_ref, acc_ref, *, kc, ln):
    kstep = pl.program_id(1)
    kb, n = w_ref.shape[1], w_ref.shape[2]
    ca_ref[...] = _silu(c_ref[...])

    @pl.when(kstep == 0)
    def _():
        acc_ref[...] = jnp.zeros_like(acc_ref)

    for lb in range(n // ln):
        cols = slice(lb * ln, (lb + 1) * ln)

        def body(i, acc):
            k0 = pl.multiple_of(i * kc, kc)
            prod = w_ref[0, pl.ds(k0, kc), cols] * ca_ref[pl.ds(k0, kc), :]
            return acc + prod.reshape(kc // SUBLANES, SUBLANES, ln).sum(axis=0)

        acc_ref[:, cols] = lax.fori_loop(0, kb // kc, body, acc_ref[:, cols], unroll=4)

    @pl.when(kstep == pl.num_programs(1) - 1)
    def _():
        o_ref[0] = jnp.sum(acc_ref[...], axis=0, keepdims=True) + b_ref[0]


def _mods(c, ada_w, ada_b, *, kb=512, kc=32, ln=1024):
    depth, d, n = ada_w.shape
    return pl.pallas_call(
        functools.partial(_mods_kernel, kc=kc, ln=ln),
        grid=(depth, d // kb),
        in_specs=[
            pl.BlockSpec((kb, 1), lambda i, k: (k, 0)),
            pl.BlockSpec((1, kb, n), lambda i, k: (i, k, 0)),
            pl.BlockSpec((1, 1, n), lambda i, k: (i, 0, 0)),
        ],
        out_specs=pl.BlockSpec((1, 1, n), lambda i, k: (i, 0, 0)),
        out_shape=jax.ShapeDtypeStruct((depth, 1, n), F32),
        scratch_shapes=[pltpu.VMEM((kb, 1), F32), pltpu.VMEM((SUBLANES, n), F32)],
        compiler_params=_cparams("arbitrary", "arbitrary"),
        name="adaln_mods",
    )(c.reshape(d, 1), ada_w, ada_b.reshape(depth, 1, n))


def _norm_kernel(x_ref, g_ref, sc_ref, sh_ref, o_ref):
    x = x_ref[...]
    y = x * lax.rsqrt(jnp.mean(x * x, axis=-1, keepdims=True) + EPS) * g_ref[...]
    o_ref[...] = (y * (1.0 + sc_ref[...]) + sh_ref[...]).astype(o_ref.dtype)


def _norm(x, gain, scale, shift, *, tm=512):
    s, d = x.shape
    row = pl.BlockSpec((1, d), lambda i: (0, 0))
    return pl.pallas_call(
        _norm_kernel,
        grid=(s // tm,),
        in_specs=[pl.BlockSpec((tm, d), lambda i: (i, 0)), row, row, row],
        out_specs=pl.BlockSpec((tm, d), lambda i: (i, 0)),
        out_shape=jax.ShapeDtypeStruct((s, d), BF16),
        compiler_params=_cparams("arbitrary"),
        name="rmsnorm_mod",
    )(x, gain, scale, shift)


def _mm_kernel(*refs, has_bias, has_colscale, norm_rows, norm_cols, norm_col_blocks):
    a_ref, w_ref = refs[0], refs[1]
    wb_ref = refs[-1]
    o_ref = refs[-3] if norm_rows else refs[-2]
    pos = 2

    @pl.when(pl.program_id(1) == 0)
    def _():
        wb_ref[...] = w_ref[...].astype(BF16)

    acc = jnp.dot(a_ref[...], wb_ref[...], preferred_element_type=F32)
    if has_bias:
        acc = acc + refs[pos][...]
        pos += 1
    if has_colscale:
        acc = acc * refs[pos][...]
    o_ref[...] = acc.astype(o_ref.dtype)
    if norm_rows:
        n_ref = refs[-2]
        n_ref[...] = jnp.zeros_like(n_ref)

        @pl.when(pl.program_id(0) < norm_col_blocks)
        def _():
            sq = acc * acc
            for g in range(acc.shape[1] // norm_cols):
                rs = jnp.sum(sq[:, g * norm_cols:(g + 1) * norm_cols], axis=-1, keepdims=True)
                for b in range(acc.shape[0] // norm_rows):
                    n_ref[0, 0, b:b + 1, g:g + 1] = jnp.max(rs[b * norm_rows:(b + 1) * norm_rows], axis=0,
                                                          keepdims=True)


def _matmul(a, w, layer, *, out_dtype, bias=None, colscale=None, norm_block=None, tm=1024, tn=1024):
    m, k = a.shape
    n_out = w.shape[2]
    in_specs = [
        pl.BlockSpec((tm, k), lambda j, i: (i, 0)),
        pl.BlockSpec((None, k, tn), lambda j, i: (layer, 0, j)),
    ]
    args = [a, w]
    rowspec = pl.BlockSpec((1, tn), lambda j, i: (0, j))
    for extra in (bias, colscale):
        if extra is not None:
            in_specs.append(rowspec)
            args.append(extra)
    out_specs = pl.BlockSpec((tm, tn), lambda j, i: (i, j))
    out_shape = jax.ShapeDtypeStruct((m, n_out), out_dtype)
    nr, nc, n_norm_cols = norm_block if norm_block else (0, 0, 0)
    if norm_block:
        assert tn // nc <= LANES
        out_specs = [out_specs, pl.BlockSpec((1, 1, tm // nr, LANES), lambda j, i: (j, i, 0, 0))]
        out_shape = [out_shape, jax.ShapeDtypeStruct((n_out // tn, m // tm, tm // nr, LANES), F32)]
    res = pl.pallas_call(
        functools.partial(_mm_kernel, has_bias=bias is not None, has_colscale=colscale is not None,
                          norm_rows=nr, norm_cols=nc, norm_col_blocks=n_norm_cols // tn),
        grid=(n_out // tn, m // tm),
        in_specs=in_specs,
        out_specs=out_specs,
        out_shape=out_shape,
        scratch_shapes=[pltpu.VMEM((k, tn), BF16)],
        compiler_params=_cparams("arbitrary", "arbitrary"),
        name="matmul_in",
    )(*args)
    if not norm_block:
        return res
    out, n2 = res
    n2 = n2[:, :, :, :tn // nc].transpose(0, 3, 1, 2).reshape(n_out // nc, m // nr)
    return out, n2


def _outproj_kernel(*refs, has_bias, has_pre, modulate, write_x):
    a_ref, w_ref, x_ref, gate_ref = refs[:4]
    pos = 4
    a = a_ref[...]
    if has_pre:
        a = a * lax.rsqrt(jnp.mean(a * a, axis=-1, keepdims=True) + EPS) * refs[pos + 1][...]
        a = (a * _silu(refs[pos][...])).astype(BF16)
        pos += 2
    y = jnp.dot(a, w_ref[...], preferred_element_type=F32)
    if has_bias:
        y = y + refs[pos][...]
        pos += 1
    g_ref, sc_ref, sh_ref = refs[pos:pos + 3]
    xn = x_ref[...] + gate_ref[...] * y
    if write_x:
        refs[-2][...] = xn
    hn = xn * lax.rsqrt(jnp.mean(xn * xn, axis=-1, keepdims=True) + EPS) * g_ref[...]
    if modulate:
        hn = hn * (1.0 + sc_ref[...]) + sh_ref[...]
    refs[-1][...] = hn.astype(refs[-1].dtype)


def _outproj(a, w, x, gate, bias, gain, scale, shift, *, modulate, write_x, out_dtype, pre=None, tm=None):
    m, k = a.shape
    d = w.shape[1]
    if tm is None:
        tm = 512
    row = pl.BlockSpec((1, d), lambda i: (0, 0))
    tile = pl.BlockSpec((tm, d), lambda i: (i, 0))
    w_spec = pl.BlockSpec((k, d), lambda i: (0, 0), pipeline_mode=pl.Buffered(1))
    in_specs = [pl.BlockSpec((tm, k), lambda i: (i, 0)), w_spec, tile, row]
    args = [a, w, x, gate]
    if pre is not None:
        u, col_block, pre_gain = pre
        in_specs += [pl.BlockSpec((tm, k), lambda i: (i, col_block)), pl.BlockSpec((1, k), lambda i: (0, 0))]
        args += [u, pre_gain]
    if bias is not None:
        in_specs.append(row)
        args.append(bias)
    in_specs += [row, row, row]
    args += [gain, scale, shift]
    out_specs = [tile]
    out_shape = [jax.ShapeDtypeStruct((m, d), out_dtype)]
    if write_x:
        out_specs = [tile, tile]
        out_shape = [jax.ShapeDtypeStruct((m, d), F32)] + out_shape
    return pl.pallas_call(
        functools.partial(_outproj_kernel, has_bias=bias is not None, has_pre=pre is not None,
                          modulate=modulate, write_x=write_x),
        grid=(m // tm,),
        in_specs=in_specs,
        out_specs=out_specs,
        out_shape=out_shape,
        compiler_params=_cparams("arbitrary"),
        name="matmul_out",
    )(*args)


def _attn_kernel(slope_ref, lam_ref, qn_ref, kn_ref, q_ref, k_ref, v_ref, g_ref, sg_ref, o_ref,
                 acc_ref, l_ref, s_ref, *, tq, out_scale):
    h = pl.program_id(0)
    dk = DA_QK_DIM
    slope = slope_ref[h]
    lam = lam_ref[0]
    lanes = LANES

    def scores(mi, k0, tk, m_old, diag_col):
        colf = lax.broadcasted_iota(jnp.int32, (1, tk), 1).astype(F32)
        bias = slope * (colf + (k0 - q0).astype(F32))
        qm = q_ref[rows, mi * dk:(mi + 1) * dk]
        km = k_ref[pl.ds(k0, tk), mi * dk:(mi + 1) * dk]
        s = lax.dot_general(qm, km, (((1,), (1,)), ((), ())), preferred_element_type=F32) + bias
        if diag_col is not None:
            r = lax.broadcasted_iota(jnp.int32, (tq, tk), 0)
            c = lax.broadcasted_iota(jnp.int32, (tq, tk), 1)
            s = jnp.where(r + diag_col >= c, s, -jnp.inf)
        s_ref[mi, :, 0:tk] = s
        smax = jnp.max(s, axis=-1, keepdims=True)
        return smax if m_old is None else jnp.maximum(m_old, smax)

    def accumulate(mi, k0, tk, m_old, m_new):
        vb = v_ref[pl.ds(k0, tk), :]
        p = jnp.exp2(s_ref[mi, :, 0:tk] - m_new)
        psum = p[:, 0:lanes]
        for t in range(1, tk // lanes):
            psum = psum + p[:, t * lanes:(t + 1) * lanes]
        if m_old is None:
            l_ref[sub, mi] = psum
            acc_ref[sub, mi] = jnp.dot(p.astype(BF16), vb, preferred_element_type=F32)
        else:
            alpha = jnp.exp2(m_old - m_new)
            l_ref[sub, mi] = alpha * l_ref[sub, mi] + psum
            acc_ref[sub, mi] = alpha * acc_ref[sub, mi] + jnp.dot(p.astype(BF16), vb,
                                                                    preferred_element_type=F32)

    def tile(k0, tk, m, diag_col=None):
        k0 = pl.multiple_of(k0, tq)
        old = (None, None) if m is None else m
        new = tuple(scores(mi, k0, tk, old[mi], diag_col) for mi in range(2))
        for mi in range(2):
            accumulate(mi, k0, tk, old[mi], new[mi])
        return new

    def dead(k0, tk):
        b0 = k0 // tq
        top_bias = slope * (k0 + (tk - 1) - q0).astype(F32)
        ok = None
        for mi in range(2):
            kmax = kn_ref[2 * h + mi, b0]
            for t in range(1, tk // tq):
                kmax = jnp.maximum(kmax, kn_ref[2 * h + mi, b0 + t])
            below = qn[mi] * kmax + top_bias < floor[mi]
            ok = below if ok is None else jnp.logical_and(ok, below)
        return ok

    def visit(k0, w, m):
        half = w // 2
        near_only = functools.partial(lambda m, k0, half: tile(k0 + half, half, m), k0=k0, half=half)
        whole = functools.partial(lambda m, k0, w: tile(k0, w, m), k0=k0, w=w)
        return lax.cond(dead(k0, w), lambda m: m, lambda m: lax.cond(dead(k0, half), near_only, whole, m), m)

    wide = s_ref.shape[2]
    n_sub = wide // tq
    for sub in range(n_sub):
        i = n_sub * pl.program_id(1) + sub
        q0 = i * tq
        rows = slice(sub * tq, (sub + 1) * tq)
        odd = sub % 2
        m = tile(q0 - tq, 2 * tq, None, diag_col=tq) if odd else tile(q0, tq, None, diag_col=0)
        qn = [qn_ref[2 * h + mi, i] * ATT_NORM_MARGIN for mi in range(2)]
        floor = [jnp.min(m[mi]) - ATT_SKIP_LOG2 for mi in range(2)]
        w = 2 * tq
        end = q0 - odd * tq
        while w < wide:
            if (sub // (w // tq)) % 2:
                m = visit(end - w, w, m)
                end = end - w
            w *= 2
        lax.fori_loop(0, pl.program_id(1), lambda jj, m: visit(jj * wide, wide, m), m)

        inv1 = 1.0 / jnp.sum(l_ref[sub, 0], axis=-1, keepdims=True)
        inv2 = lam / jnp.sum(l_ref[sub, 1], axis=-1, keepdims=True)
        o = acc_ref[sub, 0] * inv1 - acc_ref[sub, 1] * inv2
        y = o * lax.rsqrt(jnp.mean(o * o, axis=-1, keepdims=True) + EPS) * sg_ref[...] * out_scale
        o_ref[rows, :] = (y * _silu(g_ref[rows, :].astype(F32))).astype(o_ref.dtype)


ATT_TQ = 512


def _attention(u, norms, slopes, lam, subln_g, *, out_scale, tq=ATT_TQ, tk_wide=2048):
    s = u.shape[0]
    dv = DA_V_DIM
    nh = DA_HEADS
    smem = pl.BlockSpec(memory_space=pltpu.SMEM)
    n_sub = tk_wide // tq
    return pl.pallas_call(
        functools.partial(_attn_kernel, tq=tq, out_scale=out_scale),
        grid=(nh, s // tk_wide),
        in_specs=[
            smem, smem, smem, smem,
            pl.BlockSpec((tk_wide, dv), lambda h, g: (g, h)),
            pl.BlockSpec((s, dv), lambda h, g: (0, nh + h)),
            pl.BlockSpec((s, dv), lambda h, g: (0, 2 * nh + h)),
            pl.BlockSpec((tk_wide, dv), lambda h, g: (g, 3 * nh + h)),
            pl.BlockSpec((1, dv), lambda h, g: (0, 0)),
        ],
        out_specs=pl.BlockSpec((tk_wide, dv), lambda h, g: (g, h)),
        out_shape=jax.ShapeDtypeStruct((s, nh * dv), BF16),
        scratch_shapes=[pltpu.VMEM((n_sub, 2, tq, dv), F32), pltpu.VMEM((n_sub, 2, tq, LANES), F32),
                        pltpu.VMEM((2, tq, tk_wide), F32)],
        compiler_params=_cparams("arbitrary", "arbitrary"),
        name="diff_attention",
    )(slopes, lam, norms[:2 * nh], norms[2 * nh:4 * nh], u, u, u, u, subln_g)


def _hgrn_exponent_matrix():
    c = HG_CHUNK
    w = np.zeros((len(HG_LEVELS) + 1, c, c), np.float32)
    for r in range(c):
        w[0, r, :r + 1] = 1.0
        for li, m in enumerate(HG_LEVELS):
            mid = (r // (2 * m)) * 2 * m + m
            if r % (2 * m) >= m:
                w[li + 1, r, mid:r + 1] = 1.0
            else:
                w[li + 1, r, r + 1:mid] = 1.0
    return w.reshape(-1, c)


def _hgrn_kernel(q_ref, f_ref, i_ref, lb_ref, w_ref, o_ref, st_ref, st0_ref, *, tt):
    c = HG_CHUNK

    @pl.when(pl.program_id(1) == 0)
    def _():
        st_ref[...] = jnp.zeros_like(st_ref)

    lb = lb_ref[...]
    wall = w_ref[...]
    row = lax.broadcasted_iota(jnp.int32, (c, 1), 0)
    r2 = lax.broadcasted_iota(jnp.int32, (c, c), 0)
    c2 = lax.broadcasted_iota(jnp.int32, (c, c), 1)
    row8 = lax.broadcasted_iota(jnp.int32, (HG_LEVELS[-1], 1), 0)
    nt = (((1,), (1,)), ((), ()))
    tn = (((0,), (0,)), ((), ()))

    def chunk(ci, carry):
        r0 = pl.multiple_of(ci * c, c)
        q = _silu(q_ref[pl.ds(r0, c), :])
        v = i_ref[pl.ds(r0, c), :]
        fg = lb + (1.0 - lb) * jax.nn.sigmoid(f_ref[pl.ds(r0, c), :])
        kk = 1.0 - fg
        g = jnp.log(fg)
        g_hi = g.astype(BF16)
        g_lo = (g - g_hi.astype(F32)).astype(BF16)
        ex = (jnp.dot(wall, g_hi, preferred_element_type=F32)
              + jnp.dot(wall, g_lo, preferred_element_type=F32))
        b = ex[0:c]
        st = st_ref[...]
        vb = v.astype(BF16)

        o = lax.dot_general((q * jnp.exp(b)).astype(BF16), st.astype(BF16), nt,
                            preferred_element_type=F32)

        a = jnp.zeros((c, c), F32)
        for li, m in enumerate(HG_LEVELS):
            e = jnp.exp(ex[(li + 1) * c:(li + 2) * c])
            upper = (row % (2 * m)) >= m
            qt = jnp.where(upper, q * e, 0.0).astype(BF16)
            kt = jnp.where(upper, 0.0, kk * e).astype(BF16)
            al = lax.dot_general(qt, kt, nt, preferred_element_type=F32)
            if 2 * m != c:
                al = jnp.where((r2 // (2 * m)) == (c2 // (2 * m)), al, 0.0)
            a = a + al
        o = o + jnp.dot(a.astype(BF16), vb, preferred_element_type=F32)

        parts = []
        grp = HG_LEVELS[-1]
        for blk in range(c // grp):
            sl = slice(grp * blk, grp * (blk + 1))
            bb, kb, v8, q8 = b[sl], kk[sl], v[sl], q[sl]
            od = jnp.zeros((grp, HG_DIM), F32)
            for s in range(grp):
                e = jnp.exp(jnp.where(row8 >= s, bb - bb[s:s + 1], -jnp.inf))
                w = jnp.sum(e * q8 * kb[s:s + 1], axis=-1, keepdims=True)
                od = od + w * v8[s:s + 1]
            parts.append(od)
        o_ref[pl.ds(r0, c), :] = o + jnp.concatenate(parts, axis=0)

        b_last = b[c - 1:c]
        kd = (kk * jnp.exp(b_last - b)).astype(BF16)
        st_ref[...] = jnp.exp(b_last) * st + lax.dot_general(vb, kd, tn, preferred_element_type=F32)
        return carry

    n_chunks = tt // c
    dk = HG_DIM
    st0_ref[...] = st_ref[...]

    def side_by_side(fn):
        return jnp.concatenate([fn(n * c) for n in range(n_chunks)], axis=1)

    fg = side_by_side(lambda r0: lb + (1.0 - lb) * jax.nn.sigmoid(f_ref[r0:r0 + c, :]))
    g = jnp.log(fg)
    g_hi = g.astype(BF16)
    g_lo = (g - g_hi.astype(F32)).astype(BF16)
    tril = wall[0:c]
    b = (jnp.dot(tril, g_hi, preferred_element_type=F32)
         + jnp.dot(tril, g_lo, preferred_element_type=F32))
    safe = jnp.min(b[c - 1:c, :]) >= -HG_SAFE_EXP
    eb = jnp.exp(b)
    eb_last = eb[c - 1:c, :]
    qb = (side_by_side(lambda r0: _silu(q_ref[r0:r0 + c, :])) * eb).astype(BF16)
    ke = (1.0 - fg) * jnp.exp(-b)
    keb = ke.astype(BF16)
    kd = (ke * eb_last).astype(BF16)
    vbs = [i_ref[n * c:(n + 1) * c, :].astype(BF16) for n in range(n_chunks)]
    lanes_of = lambda x, n: x[:, n * dk:(n + 1) * dk]
    causal = r2 >= c2
    a = [jnp.where(causal, lax.dot_general(lanes_of(qb, n), lanes_of(keb, n), nt, preferred_element_type=F32),
                   0.0).astype(BF16) for n in range(n_chunks)]
    o_intra = [jnp.dot(a[n], vbs[n], preferred_element_type=F32) for n in range(n_chunks)]
    upd = [lax.dot_general(vbs[n], lanes_of(kd, n), tn, preferred_element_type=F32) for n in range(n_chunks)]
    st = st_ref[...]
    for n in range(n_chunks):
        o_ref[n * c:(n + 1) * c, :] = o_intra[n] + lax.dot_general(
            lanes_of(qb, n), st.astype(BF16), nt, preferred_element_type=F32)
        st = lanes_of(eb_last, n) * st + upd[n]
    st_ref[...] = st

    @pl.when(jnp.logical_not(safe))
    def _():
        st_ref[...] = st0_ref[...]
        lax.fori_loop(0, n_chunks, chunk, 0, unroll=2)


def _hgrn_core(u, lb, *, tt=2048):
    s = u.shape[0]
    nh, dk = HG_HEADS, HG_DIM
    wall = jnp.asarray(_hgrn_exponent_matrix(), BF16)
    return pl.pallas_call(
        functools.partial(_hgrn_kernel, tt=tt),
        grid=(nh, s // tt),
        in_specs=[
            pl.BlockSpec((tt, dk), lambda h, t: (t, h)),
            pl.BlockSpec((tt, dk), lambda h, t: (t, nh + h)),
            pl.BlockSpec((tt, dk), lambda h, t: (t, 2 * nh + h)),
            pl.BlockSpec((1, dk), lambda h, t: (0, h)),
            pl.BlockSpec(wall.shape, lambda h, t: (0, 0)),
        ],
        out_specs=pl.BlockSpec((tt, dk), lambda h, t: (t, h)),
        out_shape=jax.ShapeDtypeStruct((s, nh * dk), F32),
        scratch_shapes=[pltpu.VMEM((dk, dk), F32), pltpu.VMEM((dk, dk), F32)],
        compiler_params=_cparams("arbitrary", "arbitrary"),
        name="hgrn2_core",
    )(u, u, u, lb, wall)


def _conv_kernel(a_ref, ag_ref, gate_ref, dw_ref, dwb_ref, lng_ref, lnb_ref, o_ref, ybuf, cbuf, shbuf,
                 *, tt, cw):
    halo = CONV_HALO
    d = a_ref.shape[1]

    @pl.when(pl.program_id(0) == 0)
    def _():
        ybuf[0:halo, :] = jnp.zeros((halo, d), F32)

    @pl.when(pl.program_id(0) > 0)
    def _():
        ybuf[0:halo, :] = ybuf[tt:tt + halo, :]

    strip = 16

    def glu(si, carry):
        r = pl.multiple_of(si * strip, strip)
        ybuf[pl.ds(pl.multiple_of(halo + r, strip), strip), :] = (
            a_ref[pl.ds(r, strip), :] * _sigmoid(ag_ref[pl.ds(r, strip), :]))
        return carry

    lax.fori_loop(0, tt // strip, glu, 0)
    first = halo - (CONV_WIDTH - 1)

    def lanes(ci, carry):
        c0 = pl.multiple_of(ci * cw, cw)
        acc = jnp.broadcast_to(dwb_ref[:, pl.ds(c0, cw)], (tt, cw))
        for rho in range(SUBLANES):
            taps = [k for k in range(CONV_WIDTH) if (first + k) % SUBLANES == rho]
            if not taps:
                continue
            span = tt + SUBLANES * max((first + k) // SUBLANES for k in taps)
            if rho:
                shbuf[rho, 0:span, :] = ybuf[rho:rho + span, pl.ds(c0, cw)]
            for k in taps:
                a8 = SUBLANES * ((first + k) // SUBLANES)
                yk = shbuf[rho, a8:a8 + tt, :] if rho else ybuf[a8:a8 + tt, pl.ds(c0, cw)]
                acc = acc + yk * dw_ref[k:k + 1, pl.ds(c0, cw)]
        cbuf[:, pl.ds(c0, cw)] = acc
        return carry

    lax.fori_loop(0, d // cw, lanes, 0)

    def layernorm_gate(si, carry):
        r = pl.multiple_of(si * strip, strip)
        y = cbuf[pl.ds(r, strip), :]
        mu = jnp.mean(y, axis=-1, keepdims=True)
        yc = y - mu
        var = jnp.mean(yc * yc, axis=-1, keepdims=True)
        z = yc * lax.rsqrt(var + EPS) * lng_ref[...] + lnb_ref[...]
        o_ref[pl.ds(r, strip), :] = (_silu(z) * _silu(gate_ref[pl.ds(r, strip), :])).astype(o_ref.dtype)
        return carry

    lax.fori_loop(0, tt // strip, layernorm_gate, 0, unroll=4)


def _conv_core(u, dw, dw_b, ln_g, ln_b, *, tt=256, cw=LANES):
    s = u.shape[0]
    d = dw.shape[1]
    row = pl.BlockSpec((1, d), lambda t: (0, 0))
    return pl.pallas_call(
        functools.partial(_conv_kernel, tt=tt, cw=cw),
        grid=(s // tt,),
        in_specs=[
            pl.BlockSpec((tt, d), lambda t: (t, 0)),
            pl.BlockSpec((tt, d), lambda t: (t, 1)),
            pl.BlockSpec((tt, d), lambda t: (t, 2)),
            pl.BlockSpec(dw.shape, lambda t: (0, 0)),
            row, row, row,
        ],
        out_specs=pl.BlockSpec((tt, d), lambda t: (t, 0)),
        out_shape=jax.ShapeDtypeStruct((s, d), BF16),
        scratch_shapes=[pltpu.VMEM((tt + CONV_HALO, d), F32), pltpu.VMEM((tt, d), F32),
                        pltpu.VMEM((SUBLANES, tt + CONV_HALO, cw), F32)],
        compiler_params=_cparams("arbitrary"),
        name="conv_module_core",
    )(u, u, u, dw, dw_b, ln_g, ln_b)


def kernel(x, c, norm_g, ada_w, ada_b, attn_w_in, attn_w_out, attn_lam_q1, attn_lam_k1, attn_lam_q2, attn_lam_k2, attn_subln_g, hgrn_w_in, hgrn_w_out, hgrn_lb_logits, hgrn_gnorm_g, conv_w_in, conv_b_in, conv_dw, conv_dw_b, conv_ln_g, conv_ln_b, conv_w_out, conv_b_out, final_g):
    bsz, s, d = x.shape
    depth = norm_g.shape[0]
    assert bsz == 1 and c.shape == (1, d)
    xs = x.reshape(s, d)
    mods = _mods(c, ada_w, ada_b)

    slopes = (2.0 ** (-8.0 * jnp.arange(1, DA_HEADS + 1, dtype=F32) / DA_HEADS)) * LOG2E
    qk_cols = 2 * DA_HEADS * DA_QK_DIM
    attn_colscale = jnp.concatenate(
        [jnp.full((1, qk_cols), DA_QK_DIM ** -0.5 * LOG2E, F32), jnp.ones((1, 3 * qk_cols), F32)], axis=1)
    lb_all = jax.nn.softmax(hgrn_lb_logits.astype(F32), axis=0)
    lb_all = jnp.cumsum(lb_all, axis=0) - lb_all[0]

    def mod_rows(i):
        return tuple(mods[i, :, k * d:(k + 1) * d] for k in range(3))

    shift, scale, gate = mod_rows(0)
    h = _norm(xs, norm_g[0:1], scale, shift)
    for i in range(depth):
        kind, j = i % N_MIXERS, i // N_MIXERS
        bias_out = pre = None
        if kind == 0:
            u, n2 = _matmul(h, attn_w_in, j, out_dtype=BF16, colscale=attn_colscale,
                            norm_block=(ATT_TQ, DA_QK_DIM, 2 * qk_cols))
            lam_init = 0.8 - 0.6 * math.exp(-0.3 * i)
            lam = (jnp.exp(jnp.sum(attn_lam_q1[j] * attn_lam_k1[j]))
                   - jnp.exp(jnp.sum(attn_lam_q2[j] * attn_lam_k2[j])) + lam_init).reshape(1)
            y = _attention(u, jnp.sqrt(n2), slopes, lam, attn_subln_g[j:j + 1], out_scale=1.0 - lam_init)
            w_out = attn_w_out[j]
        elif kind == 1:
            u = _matmul(h, hgrn_w_in, j, out_dtype=F32)
            y = _hgrn_core(u, lb_all[i:i + 1])
            pre = (u, u.shape[1] // d - 1, hgrn_gnorm_g[j:j + 1])
            w_out = hgrn_w_out[j]
        else:
            u = _matmul(h, conv_w_in, j, out_dtype=F32, bias=conv_b_in[j:j + 1])
            y = _conv_core(u, conv_dw[j], conv_dw_b[j:j + 1], conv_ln_g[j:j + 1], conv_ln_b[j:j + 1])
            w_out = conv_w_out[j]
            bias_out = conv_b_out[j:j + 1]
        if i + 1 < depth:
            shift_n, scale_n, gate_n = mod_rows(i + 1)
            xs, h = _outproj(y, w_out.astype(BF16), xs, gate, bias_out, norm_g[i + 1:i + 2], scale_n, shift_n,
                             modulate=True, write_x=True, out_dtype=BF16, pre=pre)
            gate = gate_n
        else:
            zero = jnp.zeros((1, d), F32)
            out, = _outproj(y, w_out.astype(BF16), xs, gate, bias_out, final_g.reshape(1, d), zero, zero,
                            modulate=False, write_x=False, out_dtype=F32, pre=pre)
    return out.reshape(bsz, s, d)
```

```python
import functools
import math

import numpy as np
import jax
import jax.numpy as jnp
from jax import lax
from jax.experimental import pallas as pl
from jax.experimental.pallas import tpu as pltpu

F32 = jnp.float32
BF16 = jnp.bfloat16

EPS = 1e-6
LOG2E = 1.4426950408889634
N_MIXERS = 3

DA_HEADS = 8
DA_QK_DIM = 128
DA_V_DIM = 256
ATT_SKIP_LOG2 = 160.0
ATT_NORM_MARGIN = 1.01
HG_HEADS = 16
HG_DIM = 128
HG_CHUNK = 64
HG_LEVELS = (32, 16, 8)
HG_SAFE_EXP = 80.0
CONV_WIDTH = 31
CONV_HALO = 32

SUBLANES = 8
LANES = 128
VMEM_LIMIT_BYTES = 56 * 1024 * 1024


def _cparams(*sem):
    return pltpu.CompilerParams(dimension_semantics=sem, vmem_limit_bytes=VMEM_LIMIT_BYTES)


def _sigmoid(x):
    return 0.5 * jnp.tanh(0.5 * x) + 0.5


def _silu(x):
    h = 0.5 * x
    return h + h * jnp.tanh(h)


def _mods_kernel(c_ref, w_ref, b_ref, o_ref, ca_ref, acc_ref, *, kc, ln):
    kstep = pl.program_id(1)
    kb, n = w_ref.shape[1], w_ref.shape[2]
    ca_ref[...] = _silu(c_ref[...])

    @pl.when(kstep == 0)
    def _():
        acc_ref[...] = jnp.zeros_like(acc_ref)

    for lb in range(n // ln):
        cols = slice(lb * ln, (lb + 1) * ln)

        def body(i, acc):
            k0 = pl.multiple_of(i * kc, kc)
            prod = w_ref[0, pl.ds(k0, kc), cols] * ca_ref[pl.ds(k0, kc), :]
            return acc + prod.reshape(kc // SUBLANES, SUBLANES, ln).sum(axis=0)

        acc_ref[:, cols] = lax.fori_loop(0, kb // kc, body, acc_ref[:, cols], unroll=4)

    @pl.when(kstep == pl.num_programs(1) - 1)
    def _():
        o_ref[0] = jnp.sum(acc_ref[...], axis=0, keepdims=True) + b_ref[0]


def _mods(c, ada_w, ada_b, *, kb=512, kc=32, ln=1024):
    depth, d, n = ada_w.shape
    return pl.pallas_call(
        functools.partial(_mods_kernel, kc=kc, ln=ln),
        grid=(depth, d // kb),
        in_specs=[
            pl.BlockSpec((kb, 1), lambda i, k: (k, 0)),
            pl.BlockSpec((1, kb, n), lambda i, k: (i, k, 0)),
            pl.BlockSpec((1, 1, n), lambda i, k: (i, 0, 0)),
        ],
        out_specs=pl.BlockSpec((1, 1, n), lambda i, k: (i, 0, 0)),
        out_shape=jax.ShapeDtypeStruct((depth, 1, n), F32),
        scratch_shapes=[pltpu.VMEM((kb, 1), F32), pltpu.VMEM((SUBLANES, n), F32)],
        compiler_params=_cparams("arbitrary", "arbitrary"),
        name="adaln_mods",
    )(c.reshape(d, 1), ada_w, ada_b.reshape(depth, 1, n))


def _norm_kernel(x_ref, g_ref, sc_ref, sh_ref, o_ref):
    x = x_ref[...]
    y = x * lax.rsqrt(jnp.mean(x * x, axis=-1, keepdims=True) + EPS) * g_ref[...]
    o_ref[...] = (y * (1.0 + sc_ref[...]) + sh_ref[...]).astype(o_ref.dtype)


def _norm(x, gain, scale, shift, *, tm=512):
    s, d = x.shape
    row = pl.BlockSpec((1, d), lambda i: (0, 0))
    return pl.pallas_call(
        _norm_kernel,
        grid=(s // tm,),
        in_specs=[pl.BlockSpec((tm, d), lambda i: (i, 0)), row, row, row],
        out_specs=pl.BlockSpec((tm, d), lambda i: (i, 0)),
        out_shape=jax.ShapeDtypeStruct((s, d), BF16),
        compiler_params=_cparams("arbitrary"),
        name="rmsnorm_mod",
    )(x, gain, scale, shift)


def _mm_kernel(*refs, has_bias, has_colscale):
    a_ref, w_ref = refs[0], refs[1]
    o_ref, wb_ref = refs[-2], refs[-1]
    pos = 2

    @pl.when(pl.program_id(1) == 0)
    def _():
        wb_ref[...] = w_ref[...].astype(BF16)

    acc = jnp.dot(a_ref[...], wb_ref[...], preferred_element_type=F32)
    if has_bias:
        acc = acc + refs[pos][...]
        pos += 1
    if has_colscale:
        acc = acc * refs[pos][...]
    o_ref[...] = acc.astype(o_ref.dtype)


def _matmul(a, w, layer, *, out_dtype, bias=None, colscale=None, tm=1024, tn=1024):
    m, k = a.shape
    n_out = w.shape[2]
    in_specs = [
        pl.BlockSpec((tm, k), lambda j, i: (i, 0)),
        pl.BlockSpec((None, k, tn), lambda j, i: (layer, 0, j)),
    ]
    args = [a, w]
    rowspec = pl.BlockSpec((1, tn), lambda j, i: (0, j))
    for extra in (bias, colscale):
        if extra is not None:
            in_specs.append(rowspec)
            args.append(extra)
    return pl.pallas_call(
        functools.partial(_mm_kernel, has_bias=bias is not None, has_colscale=colscale is not None),
        grid=(n_out // tn, m // tm),
        in_specs=in_specs,
        out_specs=pl.BlockSpec((tm, tn), lambda j, i: (i, j)),
        out_shape=jax.ShapeDtypeStruct((m, n_out), out_dtype),
        scratch_shapes=[pltpu.VMEM((k, tn), BF16)],
        compiler_params=_cparams("arbitrary", "arbitrary"),
        name="matmul_in",
    )(*args)


def _outproj_kernel(*refs, has_bias, has_pre, modulate, write_x):
    a_ref, w_ref, x_ref, gate_ref = refs[:4]
    pos = 4
    a = a_ref[...]
    if has_pre:
        a = a * lax.rsqrt(jnp.mean(a * a, axis=-1, keepdims=True) + EPS) * refs[pos + 1][...]
        a = (a * _silu(refs[pos][...])).astype(BF16)
        pos += 2
    y = jnp.dot(a, w_ref[...], preferred_element_type=F32)
    if has_bias:
        y = y + refs[pos][...]
        pos += 1
    g_ref, sc_ref, sh_ref = refs[pos:pos + 3]
    xn = x_ref[...] + gate_ref[...] * y
    if write_x:
        refs[-2][...] = xn
    hn = xn * lax.rsqrt(jnp.mean(xn * xn, axis=-1, keepdims=True) + EPS) * g_ref[...]
    if modulate:
        hn = hn * (1.0 + sc_ref[...]) + sh_ref[...]
    refs[-1][...] = hn.astype(refs[-1].dtype)


def _outproj(a, w, x, gate, bias, gain, scale, shift, *, modulate, write_x, out_dtype, pre=None, tm=None):
    m, k = a.shape
    d = w.shape[1]
    if tm is None:
        tm = 512
    row = pl.BlockSpec((1, d), lambda i: (0, 0))
    tile = pl.BlockSpec((tm, d), lambda i: (i, 0))
    w_spec = pl.BlockSpec((k, d), lambda i: (0, 0), pipeline_mode=pl.Buffered(1))
    in_specs = [pl.BlockSpec((tm, k), lambda i: (i, 0)), w_spec, tile, row]
    args = [a, w, x, gate]
    if pre is not None:
        u, col_block, pre_gain = pre
        in_specs += [pl.BlockSpec((tm, k), lambda i: (i, col_block)), pl.BlockSpec((1, k), lambda i: (0, 0))]
        args += [u, pre_gain]
    if bias is not None:
        in_specs.append(row)
        args.append(bias)
    in_specs += [row, row, row]
    args += [gain, scale, shift]
    out_specs = [tile]
    out_shape = [jax.ShapeDtypeStruct((m, d), out_dtype)]
    if write_x:
        out_specs = [tile, tile]
        out_shape = [jax.ShapeDtypeStruct((m, d), F32)] + out_shape
    return pl.pallas_call(
        functools.partial(_outproj_kernel, has_bias=bias is not None, has_pre=pre is not None,
                          modulate=modulate, write_x=write_x),
        grid=(m // tm,),
        in_specs=in_specs,
        out_specs=out_specs,
        out_shape=out_shape,
        compiler_params=_cparams("arbitrary"),
        name="matmul_out",
    )(*args)


def _attn_kernel(slope_ref, lam_ref, qn_ref, kn_ref, q_ref, k_ref, v_ref, g_ref, sg_ref, o_ref,
                 acc_ref, l_ref, s_ref, *, tq, out_scale):
    h = pl.program_id(0)
    dk = DA_QK_DIM
    slope = slope_ref[h]
    lam = lam_ref[0]
    lanes = LANES

    def scores(mi, k0, tk, m_old, diag_col):
        colf = lax.broadcasted_iota(jnp.int32, (1, tk), 1).astype(F32)
        bias = slope * (colf + (k0 - q0).astype(F32))
        qm = q_ref[rows, mi * dk:(mi + 1) * dk]
        km = k_ref[pl.ds(k0, tk), mi * dk:(mi + 1) * dk]
        s = lax.dot_general(qm, km, (((1,), (1,)), ((), ())), preferred_element_type=F32) + bias
        if diag_col is not None:
            r = lax.broadcasted_iota(jnp.int32, (tq, tk), 0)
            c = lax.broadcasted_iota(jnp.int32, (tq, tk), 1)
            s = jnp.where(r + diag_col >= c, s, -jnp.inf)
        s_ref[mi, :, 0:tk] = s
        smax = jnp.max(s, axis=-1, keepdims=True)
        return smax if m_old is None else jnp.maximum(m_old, smax)

    def accumulate(mi, k0, tk, m_old, m_new):
        vb = v_ref[pl.ds(k0, tk), :]
        p = jnp.exp2(s_ref[mi, :, 0:tk] - m_new)
        psum = p[:, 0:lanes]
        for t in range(1, tk // lanes):
            psum = psum + p[:, t * lanes:(t + 1) * lanes]
        if m_old is None:
            l_ref[sub, mi] = psum
            acc_ref[sub, mi] = jnp.dot(p.astype(BF16), vb, preferred_element_type=F32)
        else:
            alpha = jnp.exp2(m_old - m_new)
            l_ref[sub, mi] = alpha * l_ref[sub, mi] + psum
            acc_ref[sub, mi] = alpha * acc_ref[sub, mi] + jnp.dot(p.astype(BF16), vb,
                                                                    preferred_element_type=F32)

    def tile(k0, tk, m, diag_col=None):
        k0 = pl.multiple_of(k0, tq)
        old = (None, None) if m is None else m
        new = tuple(scores(mi, k0, tk, old[mi], diag_col) for mi in range(2))
        for mi in range(2):
            accumulate(mi, k0, tk, old[mi], new[mi])
        return new

    def dead(k0, tk):
        b0 = k0 // tq
        top_bias = slope * (k0 + (tk - 1) - q0).astype(F32)
        ok = None
        for mi in range(2):
            kmax = kn_ref[2 * h + mi, b0]
            for t in range(1, tk // tq):
                kmax = jnp.maximum(kmax, kn_ref[2 * h + mi, b0 + t])
            below = qn[mi] * kmax + top_bias < floor[mi]
            ok = below if ok is None else jnp.logical_and(ok, below)
        return ok

    def visit(k0, w, m):
        half = w // 2
        near_only = functools.partial(lambda m, k0, half: tile(k0 + half, half, m), k0=k0, half=half)
        whole = functools.partial(lambda m, k0, w: tile(k0, w, m), k0=k0, w=w)
        return lax.cond(dead(k0, w), lambda m: m, lambda m: lax.cond(dead(k0, half), near_only, whole, m), m)

    wide = s_ref.shape[2]
    n_sub = wide // tq
    for sub in range(n_sub):
        i = n_sub * pl.program_id(1) + sub
        q0 = i * tq
        rows = slice(sub * tq, (sub + 1) * tq)
        odd = sub % 2
        m = tile(q0 - tq, 2 * tq, None, diag_col=tq) if odd else tile(q0, tq, None, diag_col=0)
        qn = [qn_ref[2 * h + mi, i] * ATT_NORM_MARGIN for mi in range(2)]
        floor = [jnp.min(m[mi]) - ATT_SKIP_LOG2 for mi in range(2)]
        w = 2 * tq
        end = q0 - odd * tq
        while w < wide:
            if (sub // (w // tq)) % 2:
                m = visit(end - w, w, m)
                end = end - w
            w *= 2
        lax.fori_loop(0, pl.program_id(1), lambda jj, m: visit(jj * wide, wide, m), m)

        inv1 = 1.0 / jnp.sum(l_ref[sub, 0], axis=-1, keepdims=True)
        inv2 = lam / jnp.sum(l_ref[sub, 1], axis=-1, keepdims=True)
        o = acc_ref[sub, 0] * inv1 - acc_ref[sub, 1] * inv2
        y = o * lax.rsqrt(jnp.mean(o * o, axis=-1, keepdims=True) + EPS) * sg_ref[...] * out_scale
        o_ref[rows, :] = (y * _silu(g_ref[rows, :].astype(F32))).astype(o_ref.dtype)


def _block_norm_kernel(x_ref, g_ref, o_ref):
    x = x_ref[...]
    rs = jnp.dot(x * x, g_ref[...], preferred_element_type=F32)
    o_ref[0] = jnp.sqrt(jnp.max(rs, axis=0, keepdims=True))


def _qk_block_norms(u, *, tq):
    s = u.shape[0]
    groups = 4 * DA_HEADS
    cols = groups * DA_QK_DIM
    assert groups <= LANES
    ind = np.zeros((cols, LANES), np.float32)
    ind[np.arange(cols), np.arange(cols) // DA_QK_DIM] = 1.0
    out = pl.pallas_call(
        _block_norm_kernel,
        grid=(s // tq,),
        in_specs=[pl.BlockSpec((tq, cols), lambda i: (i, 0)), pl.BlockSpec((cols, LANES), lambda i: (0, 0))],
        out_specs=pl.BlockSpec((1, 1, LANES), lambda i: (i, 0, 0)),
        out_shape=jax.ShapeDtypeStruct((s // tq, 1, LANES), F32),
        compiler_params=_cparams("arbitrary"),
        name="qk_block_norms",
    )(u, jnp.asarray(ind, BF16))
    return out[:, 0, :groups].T


def _attention(u, slopes, lam, subln_g, *, out_scale, tq=512, tk_wide=2048):
    s = u.shape[0]
    dv = DA_V_DIM
    nh = DA_HEADS
    smem = pl.BlockSpec(memory_space=pltpu.SMEM)
    norms = _qk_block_norms(u, tq=tq)
    n_sub = tk_wide // tq
    return pl.pallas_call(
        functools.partial(_attn_kernel, tq=tq, out_scale=out_scale),
        grid=(nh, s // tk_wide),
        in_specs=[
            smem, smem, smem, smem,
            pl.BlockSpec((tk_wide, dv), lambda h, g: (g, h)),
            pl.BlockSpec((s, dv), lambda h, g: (0, nh + h)),
            pl.BlockSpec((s, dv), lambda h, g: (0, 2 * nh + h)),
            pl.BlockSpec((tk_wide, dv), lambda h, g: (g, 3 * nh + h)),
            pl.BlockSpec((1, dv), lambda h, g: (0, 0)),
        ],
        out_specs=pl.BlockSpec((tk_wide, dv), lambda h, g: (g, h)),
        out_shape=jax.ShapeDtypeStruct((s, nh * dv), BF16),
        scratch_shapes=[pltpu.VMEM((n_sub, 2, tq, dv), F32), pltpu.VMEM((n_sub, 2, tq, LANES), F32),
                        pltpu.VMEM((2, tq, tk_wide), F32)],
        compiler_params=_cparams("arbitrary", "arbitrary"),
        name="diff_attention",
    )(slopes, lam, norms[:2 * nh], norms[2 * nh:], u, u, u, u, subln_g)


def _hgrn_exponent_matrix():
    c = HG_CHUNK
    w = np.zeros((len(HG_LEVELS) + 1, c, c), np.float32)
    for r in range(c):
        w[0, r, :r + 1] = 1.0
        for li, m in enumerate(HG_LEVELS):
            mid = (r // (2 * m)) * 2 * m + m
            if r % (2 * m) >= m:
                w[li + 1, r, mid:r + 1] = 1.0
            else:
                w[li + 1, r, r + 1:mid] = 1.0
    return w.reshape(-1, c)


def _hgrn_kernel(q_ref, f_ref, i_ref, lb_ref, w_ref, o_ref, st_ref, st0_ref, *, tt):
    c = HG_CHUNK

    @pl.when(pl.program_id(1) == 0)
    def _():
        st_ref[...] = jnp.zeros_like(st_ref)

    lb = lb_ref[...]
    wall = w_ref[...]
    row = lax.broadcasted_iota(jnp.int32, (c, 1), 0)
    r2 = lax.broadcasted_iota(jnp.int32, (c, c), 0)
    c2 = lax.broadcasted_iota(jnp.int32, (c, c), 1)
    row8 = lax.broadcasted_iota(jnp.int32, (HG_LEVELS[-1], 1), 0)
    nt = (((1,), (1,)), ((), ()))
    tn = (((0,), (0,)), ((), ()))

    def chunk(ci, carry):
        r0 = pl.multiple_of(ci * c, c)
        q = _silu(q_ref[pl.ds(r0, c), :])
        v = i_ref[pl.ds(r0, c), :]
        fg = lb + (1.0 - lb) * jax.nn.sigmoid(f_ref[pl.ds(r0, c), :])
        kk = 1.0 - fg
        g = jnp.log(fg)
        g_hi = g.astype(BF16)
        g_lo = (g - g_hi.astype(F32)).astype(BF16)
        ex = (jnp.dot(wall, g_hi, preferred_element_type=F32)
              + jnp.dot(wall, g_lo, preferred_element_type=F32))
        b = ex[0:c]
        st = st_ref[...]
        vb = v.astype(BF16)

        o = lax.dot_general((q * jnp.exp(b)).astype(BF16), st.astype(BF16), nt,
                            preferred_element_type=F32)

        a = jnp.zeros((c, c), F32)
        for li, m in enumerate(HG_LEVELS):
            e = jnp.exp(ex[(li + 1) * c:(li + 2) * c])
            upper = (row % (2 * m)) >= m
            qt = jnp.where(upper, q * e, 0.0).astype(BF16)
            kt = jnp.where(upper, 0.0, kk * e).astype(BF16)
            al = lax.dot_general(qt, kt, nt, preferred_element_type=F32)
            if 2 * m != c:
                al = jnp.where((r2 // (2 * m)) == (c2 // (2 * m)), al, 0.0)
            a = a + al
        o = o + jnp.dot(a.astype(BF16), vb, preferred_element_type=F32)

        parts = []
        grp = HG_LEVELS[-1]
        for blk in range(c // grp):
            sl = slice(grp * blk, grp * (blk + 1))
            bb, kb, v8, q8 = b[sl], kk[sl], v[sl], q[sl]
            od = jnp.zeros((grp, HG_DIM), F32)
            for s in range(grp):
                e = jnp.exp(jnp.where(row8 >= s, bb - bb[s:s + 1], -jnp.inf))
                w = jnp.sum(e * q8 * kb[s:s + 1], axis=-1, keepdims=True)
                od = od + w * v8[s:s + 1]
            parts.append(od)
        o_ref[pl.ds(r0, c), :] = o + jnp.concatenate(parts, axis=0)

        b_last = b[c - 1:c]
        kd = (kk * jnp.exp(b_last - b)).astype(BF16)
        st_ref[...] = jnp.exp(b_last) * st + lax.dot_general(vb, kd, tn, preferred_element_type=F32)
        return carry

    n_chunks = tt // c
    dk = HG_DIM
    st0_ref[...] = st_ref[...]

    def side_by_side(fn):
        return jnp.concatenate([fn(n * c) for n in range(n_chunks)], axis=1)

    fg = side_by_side(lambda r0: lb + (1.0 - lb) * jax.nn.sigmoid(f_ref[r0:r0 + c, :]))
    g = jnp.log(fg)
    g_hi = g.astype(BF16)
    g_lo = (g - g_hi.astype(F32)).astype(BF16)
    tril = wall[0:c]
    b = (jnp.dot(tril, g_hi, preferred_element_type=F32)
         + jnp.dot(tril, g_lo, preferred_element_type=F32))
    safe = jnp.min(b[c - 1:c, :]) >= -HG_SAFE_EXP
    eb = jnp.exp(b)
    eb_last = eb[c - 1:c, :]
    qb = (side_by_side(lambda r0: _silu(q_ref[r0:r0 + c, :])) * eb).astype(BF16)
    ke = (1.0 - fg) * jnp.exp(-b)
    keb = ke.astype(BF16)
    kd = (ke * eb_last).astype(BF16)
    vbs = [i_ref[n * c:(n + 1) * c, :].astype(BF16) for n in range(n_chunks)]
    lanes_of = lambda x, n: x[:, n * dk:(n + 1) * dk]
    causal = r2 >= c2
    a = [jnp.where(causal, lax.dot_general(lanes_of(qb, n), lanes_of(keb, n), nt, preferred_element_type=F32),
                   0.0).astype(BF16) for n in range(n_chunks)]
    o_intra = [jnp.dot(a[n], vbs[n], preferred_element_type=F32) for n in range(n_chunks)]
    upd = [lax.dot_general(vbs[n], lanes_of(kd, n), tn, preferred_element_type=F32) for n in range(n_chunks)]
    st = st_ref[...]
    for n in range(n_chunks):
        o_ref[n * c:(n + 1) * c, :] = o_intra[n] + lax.dot_general(
            lanes_of(qb, n), st.astype(BF16), nt, preferred_element_type=F32)
        st = lanes_of(eb_last, n) * st + upd[n]
    st_ref[...] = st

    @pl.when(jnp.logical_not(safe))
    def _():
        st_ref[...] = st0_ref[...]
        lax.fori_loop(0, n_chunks, chunk, 0, unroll=2)


def _hgrn_core(u, lb, *, tt=4096):
    s = u.shape[0]
    nh, dk = HG_HEADS, HG_DIM
    wall = jnp.asarray(_hgrn_exponent_matrix(), BF16)
    return pl.pallas_call(
        functools.partial(_hgrn_kernel, tt=tt),
        grid=(nh, s // tt),
        in_specs=[
            pl.BlockSpec((tt, dk), lambda h, t: (t, h)),
            pl.BlockSpec((tt, dk), lambda h, t: (t, nh + h)),
            pl.BlockSpec((tt, dk), lambda h, t: (t, 2 * nh + h)),
            pl.BlockSpec((1, dk), lambda h, t: (0, h)),
            pl.BlockSpec(wall.shape, lambda h, t: (0, 0)),
        ],
        out_specs=pl.BlockSpec((tt, dk), lambda h, t: (t, h)),
        out_shape=jax.ShapeDtypeStruct((s, nh * dk), F32),
        scratch_shapes=[pltpu.VMEM((dk, dk), F32), pltpu.VMEM((dk, dk), F32)],
        compiler_params=_cparams("arbitrary", "arbitrary"),
        name="hgrn2_core",
    )(u, u, u, lb, wall)


def _conv_kernel(a_ref, ag_ref, gate_ref, dw_ref, dwb_ref, lng_ref, lnb_ref, o_ref, ybuf, cbuf, shbuf,
                 *, tt, cw):
    halo = CONV_HALO
    d = a_ref.shape[1]

    @pl.when(pl.program_id(0) == 0)
    def _():
        ybuf[0:halo, :] = jnp.zeros((halo, d), F32)

    @pl.when(pl.program_id(0) > 0)
    def _():
        ybuf[0:halo, :] = ybuf[tt:tt + halo, :]

    strip = 16

    def glu(si, carry):
        r = pl.multiple_of(si * strip, strip)
        ybuf[pl.ds(pl.multiple_of(halo + r, strip), strip), :] = (
            a_ref[pl.ds(r, strip), :] * _sigmoid(ag_ref[pl.ds(r, strip), :]))
        return carry

    lax.fori_loop(0, tt // strip, glu, 0)
    first = halo - (CONV_WIDTH - 1)

    def lanes(ci, carry):
        c0 = pl.multiple_of(ci * cw, cw)
        acc = jnp.broadcast_to(dwb_ref[:, pl.ds(c0, cw)], (tt, cw))
        for rho in range(SUBLANES):
            taps = [k for k in range(CONV_WIDTH) if (first + k) % SUBLANES == rho]
            if not taps:
                continue
            span = tt + SUBLANES * max((first + k) // SUBLANES for k in taps)
            if rho:
                shbuf[rho, 0:span, :] = ybuf[rho:rho + span, pl.ds(c0, cw)]
            for k in taps:
                a8 = SUBLANES * ((first + k) // SUBLANES)
                yk = shbuf[rho, a8:a8 + tt, :] if rho else ybuf[a8:a8 + tt, pl.ds(c0, cw)]
                acc = acc + yk * dw_ref[k:k + 1, pl.ds(c0, cw)]
        cbuf[:, pl.ds(c0, cw)] = acc
        return carry

    lax.fori_loop(0, d // cw, lanes, 0)

    def layernorm_gate(si, carry):
        r = pl.multiple_of(si * strip, strip)
        y = cbuf[pl.ds(r, strip), :]
        mu = jnp.mean(y, axis=-1, keepdims=True)
        yc = y - mu
        var = jnp.mean(yc * yc, axis=-1, keepdims=True)
        z = yc * lax.rsqrt(var + EPS) * lng_ref[...] + lnb_ref[...]
        o_ref[pl.ds(r, strip), :] = (_silu(z) * _silu(gate_ref[pl.ds(r, strip), :])).astype(o_ref.dtype)
        return carry

    lax.fori_loop(0, tt // strip, layernorm_gate, 0, unroll=4)


def _conv_core(u, dw, dw_b, ln_g, ln_b, *, tt=256, cw=LANES):
    s = u.shape[0]
    d = dw.shape[1]
    row = pl.BlockSpec((1, d), lambda t: (0, 0))
    return pl.pallas_call(
        functools.partial(_conv_kernel, tt=tt, cw=cw),
        grid=(s // tt,),
        in_specs=[
            pl.BlockSpec((tt, d), lambda t: (t, 0)),
            pl.BlockSpec((tt, d), lambda t: (t, 1)),
            pl.BlockSpec((tt, d), lambda t: (t, 2)),
            pl.BlockSpec(dw.shape, lambda t: (0, 0)),
            row, row, row,
        ],
        out_specs=pl.BlockSpec((tt, d), lambda t: (t, 0)),
        out_shape=jax.ShapeDtypeStruct((s, d), BF16),
        scratch_shapes=[pltpu.VMEM((tt + CONV_HALO, d), F32), pltpu.VMEM((tt, d), F32),
                        pltpu.VMEM((SUBLANES, tt + CONV_HALO, cw), F32)],
        compiler_params=_cparams("arbitrary"),
        name="conv_module_core",
    )(u, u, u, dw, dw_b, ln_g, ln_b)


def kernel(x, c, norm_g, ada_w, ada_b, attn_w_in, attn_w_out, attn_lam_q1, attn_lam_k1, attn_lam_q2, attn_lam_k2, attn_subln_g, hgrn_w_in, hgrn_w_out, hgrn_lb_logits, hgrn_gnorm_g, conv_w_in, conv_b_in, conv_dw, conv_dw_b, conv_ln_g, conv_ln_b, conv_w_out, conv_b_out, final_g):
    bsz, s, d = x.shape
    depth = norm_g.shape[0]
    assert bsz == 1 and c.shape == (1, d)
    xs = x.reshape(s, d)
    mods = _mods(c, ada_w, ada_b)

    slopes = (2.0 ** (-8.0 * jnp.arange(1, DA_HEADS + 1, dtype=F32) / DA_HEADS)) * LOG2E
    qk_cols = 2 * DA_HEADS * DA_QK_DIM
    attn_colscale = jnp.concatenate(
        [jnp.full((1, qk_cols), DA_QK_DIM ** -0.5 * LOG2E, F32), jnp.ones((1, 3 * qk_cols), F32)], axis=1)
    lb_all = jax.nn.softmax(hgrn_lb_logits.astype(F32), axis=0)
    lb_all = jnp.cumsum(lb_all, axis=0) - lb_all[0]

    def mod_rows(i):
        return tuple(mods[i, :, k * d:(k + 1) * d] for k in range(3))

    shift, scale, gate = mod_rows(0)
    h = _norm(xs, norm_g[0:1], scale, shift)
    for i in range(depth):
        kind, j = i % N_MIXERS, i // N_MIXERS
        bias_out = pre = None
        if kind == 0:
            u = _matmul(h, attn_w_in, j, out_dtype=BF16, colscale=attn_colscale)
            lam_init = 0.8 - 0.6 * math.exp(-0.3 * i)
            lam = (jnp.exp(jnp.sum(attn_lam_q1[j] * attn_lam_k1[j]))
                   - jnp.exp(jnp.sum(attn_lam_q2[j] * attn_lam_k2[j])) + lam_init).reshape(1)
            y = _attention(u, slopes, lam, attn_subln_g[j:j + 1], out_scale=1.0 - lam_init)
            w_out = attn_w_out[j]
        elif kind == 1:
            u = _matmul(h, hgrn_w_in, j, out_dtype=F32)
            y = _hgrn_core(u, lb_all[i:i + 1])
            pre = (u, u.shape[1] // d - 1, hgrn_gnorm_g[j:j + 1])
            w_out = hgrn_w_out[j]
        else:
            u = _matmul(h, conv_w_in, j, out_dtype=F32, bias=conv_b_in[j:j + 1])
            y = _conv_core(u, conv_dw[j], conv_dw_b[j:j + 1], conv_ln_g[j:j + 1], conv_ln_b[j:j + 1])
            w_out = conv_w_out[j]
            bias_out = conv_b_out[j:j + 1]
        if i + 1 < depth:
            shift_n, scale_n, gate_n = mod_rows(i + 1)
            xs, h = _outproj(y, w_out.astype(BF16), xs, gate, bias_out, norm_g[i + 1:i + 2], scale_n, shift_n,
                             modulate=True, write_x=True, out_dtype=BF16, pre=pre)
            gate = gate_n
        else:
            zero = jnp.zeros((1, d), F32)
            out, = _outproj(y, w_out.astype(BF16), xs, gate, bias_out, final_g.reshape(1, d), zero, zero,
                            modulate=False, write_x=False, out_dtype=F32, pre=pre)
    return out.reshape(bsz, s, d)
```
